```python
import math
import jax, jax.numpy as jnp
from jax import lax
import numpy as np

D_MODEL = 4096
BATCH = 4
SEQ = 4096
DEPTH = 1

D_MIX = D_MODEL
CONV_CH = D_MIX // 2
CONV_GROUPS = 8
CONV_WIDTH = 31
RET_WIDTH = D_MIX - CONV_CH
RET_HEADS = 8
RET_HEAD_DIM = RET_WIDTH // RET_HEADS
RET_CHUNK = 128
ROPE_BASE = 10000.0
D_FF = int(math.ceil(D_MODEL * 8 / 3 / 256) * 256)
N_MOD = 6
EPS = 1e-6
LN_EPS = 1e-5

OFF_CONV_VAL = 0
OFF_CONV_GATE = OFF_CONV_VAL + CONV_CH
OFF_Q = OFF_CONV_GATE + CONV_CH
OFF_K = OFF_Q + RET_WIDTH
OFF_V = OFF_K + RET_WIDTH
OFF_G = OFF_V + RET_WIDTH
D_IN_PROJ = OFF_G + RET_WIDTH

kernel_name = "hymba_conformer_retnet_adaln_block"


def rms_norm(x, g):
    xf = x.astype(jnp.float32)
    y = xf * lax.rsqrt(jnp.mean(xf * xf, axis=-1, keepdims=True) + EPS)
    return (y * g.astype(jnp.float32)).astype(x.dtype)


def layer_norm(x, g, b):
    xf = x.astype(jnp.float32)
    mu = jnp.mean(xf, axis=-1, keepdims=True)
    var = jnp.mean(jnp.square(xf - mu), axis=-1, keepdims=True)
    y = (xf - mu) * lax.rsqrt(var + LN_EPS)
    return (y * g.astype(jnp.float32) + b.astype(jnp.float32)).astype(x.dtype)


def rope(t, positions):
    half = t.shape[-1] // 2
    freqs = ROPE_BASE ** (-jnp.arange(half, dtype=jnp.float32) / half)
    ang = positions.astype(jnp.float32)[:, :, None] * freqs
    cos = jnp.cos(ang)[:, :, None, :]
    sin = jnp.sin(ang)[:, :, None, :]
    t1, t2 = t[..., :half], t[..., half:]
    out = jnp.concatenate([t1 * cos - t2 * sin, t2 * cos + t1 * sin], axis=-1)
    return out.astype(t.dtype)


def conformer_conv(val, gate, dw_w, dw_b, ln_g, ln_b):
    a = val * jax.nn.sigmoid(gate)
    kern = dw_w[:, None, :].astype(a.dtype)
    y = lax.conv_general_dilated(
        a, kern, window_strides=(1,), padding=[(CONV_WIDTH - 1, 0)],
        dimension_numbers=("NWC", "WIO", "NWC"), feature_group_count=CONV_CH)
    y = y + dw_b.astype(a.dtype)
    y = layer_norm(y, ln_g, ln_b)
    return jax.nn.silu(y)


def chunkwise_retention(q, k, v):
    B, S, H, d = q.shape
    nc = S // RET_CHUNK
    log_gamma = jnp.log(1.0 - 2.0 ** (-5.0 - jnp.arange(H, dtype=jnp.float32)))

    def to_chunks(t):
        return jnp.transpose(t, (0, 2, 1, 3)).reshape(B, H, nc, RET_CHUNK, d)

    qc, kc, vc = to_chunks(q), to_chunks(k), to_chunks(v)
    idx = jnp.arange(RET_CHUNK)
    diff = (idx[:, None] - idx[None, :]).astype(jnp.float32)
    inner_decay = jnp.where(diff[None] >= 0,
                            jnp.exp(jnp.maximum(diff, 0.0)[None] * log_gamma[:, None, None]),
                            0.0)
    scores = jnp.einsum("bhncd,bhnmd->bhncm", qc, kc) * inner_decay[None, :, None]
    inner = jnp.einsum("bhncm,bhnme->bhnce", scores, vc)

    idx_f = idx.astype(jnp.float32)
    xi = jnp.exp((idx_f + 1.0)[None] * log_gamma[:, None])
    zeta = jnp.exp((RET_CHUNK - 1.0 - idx_f)[None] * log_gamma[:, None])
    chunk_decay = jnp.exp(RET_CHUNK * log_gamma)

    def step(R, inp):
        q_i, k_i, v_i = inp
        cross = jnp.einsum("bhcd,bhde->bhce", q_i * xi[None, :, :, None], R)
        R = R * chunk_decay[None, :, None, None] + jnp.einsum(
            "bhcd,bhce->bhde", k_i * zeta[None, :, :, None], v_i)
        return R, cross

    R0 = jnp.zeros((B, H, d, d), jnp.float32)
    xs = (jnp.moveaxis(qc, 2, 0), jnp.moveaxis(kc, 2, 0), jnp.moveaxis(vc, 2, 0))
    _, crosses = lax.scan(step, R0, xs)
    out = inner + jnp.moveaxis(crosses, 0, 2)
    out = out.reshape(B, H, S, d).transpose(0, 2, 1, 3)
    return out.astype(v.dtype)


def head_group_norm(y, g):
    yf = y.astype(jnp.float32)
    mu = jnp.mean(yf, axis=-1, keepdims=True)
    var = jnp.mean(jnp.square(yf - mu), axis=-1, keepdims=True)
    out = (yf - mu) * lax.rsqrt(var + LN_EPS) * g.astype(jnp.float32).reshape(RET_HEADS, RET_HEAD_DIM)
    return out.astype(y.dtype)


def setup_inputs(seed: int = 0) -> dict:
    key = jax.random.key(seed)
    ks = jax.random.split(key, 16)
    f32 = jnp.float32
    nrm = lambda k, shape, scale: jax.random.normal(k, shape, f32) * scale
    return {
        "x": nrm(ks[0], (BATCH, SEQ, D_MODEL), 1.0),
        "c": nrm(ks[1], (BATCH, D_MODEL), 1.0),
        "positions": jnp.broadcast_to(jnp.arange(SEQ, dtype=jnp.int32), (BATCH, SEQ)),
        "norm1_g": 1.0 + nrm(ks[2], (DEPTH, D_MODEL), 0.02),
        "norm2_g": 1.0 + nrm(ks[3], (DEPTH, D_MODEL), 0.02),
        "normf_g": 1.0 + nrm(ks[4], (D_MODEL,), 0.02),
        "w_mod": nrm(ks[5], (DEPTH, D_MODEL, N_MOD * D_MODEL), D_MODEL ** -0.5),
        "b_mod": nrm(ks[6], (DEPTH, N_MOD * D_MODEL), 0.02),
        "w_in": nrm(ks[7], (DEPTH, D_MODEL, D_IN_PROJ), D_MODEL ** -0.5),
        "conv_dw_w": nrm(ks[8], (DEPTH, CONV_WIDTH, CONV_CH), CONV_WIDTH ** -0.5),
        "conv_dw_b": nrm(ks[9], (DEPTH, CONV_CH), 0.02),
        "conv_ln_g": 1.0 + nrm(ks[10], (DEPTH, CONV_CH), 0.02),
        "conv_ln_b": nrm(ks[11], (DEPTH, CONV_CH), 0.02),
        "ret_gn_g": 1.0 + nrm(ks[12], (DEPTH, RET_WIDTH), 0.02),
        "w_out": nrm(ks[13], (DEPTH, D_MIX, D_MODEL), D_MIX ** -0.5),
        "w_ffn1": nrm(ks[14], (DEPTH, D_MODEL, D_FF), D_MODEL ** -0.5),
        "w_ffn3": nrm(jax.random.fold_in(ks[15], 1), (DEPTH, D_MODEL, D_FF), D_MODEL ** -0.5),
        "w_ffn2": nrm(jax.random.fold_in(ks[15], 2), (DEPTH, D_FF, D_MODEL), D_FF ** -0.5),
    }


def reference(x, c, positions, norm1_g, norm2_g, normf_g, w_mod, b_mod, w_in,
              conv_dw_w, conv_dw_b, conv_ln_g, conv_ln_b, ret_gn_g, w_out,
              w_ffn1, w_ffn3, w_ffn2):
    B, S, _ = x.shape
    c_act = jax.nn.silu(c)
    for l in range(DEPTH):
        mod = c_act @ w_mod[l] + b_mod[l]
        shift1, scale1, gate1, shift2, scale2, gate2 = [
            m[:, None, :] for m in jnp.split(mod, N_MOD, axis=-1)]

        h = rms_norm(x, norm1_g[l]) * (1.0 + scale1) + shift1
        p = h @ w_in[l]

        conv_out = conformer_conv(p[..., OFF_CONV_VAL:OFF_CONV_GATE],
                                  p[..., OFF_CONV_GATE:OFF_Q],
                                  conv_dw_w[l], conv_dw_b[l], conv_ln_g[l], conv_ln_b[l])

        q = p[..., OFF_Q:OFF_K].reshape(B, S, RET_HEADS, RET_HEAD_DIM)
        k = p[..., OFF_K:OFF_V].reshape(B, S, RET_HEADS, RET_HEAD_DIM) * (RET_HEAD_DIM ** -0.5)
        v = p[..., OFF_V:OFF_G].reshape(B, S, RET_HEADS, RET_HEAD_DIM)
        g = p[..., OFF_G:]
        q = rope(q, positions)
        k = rope(k, positions)
        ret = head_group_norm(chunkwise_retention(q, k, v), ret_gn_g[l])
        ret_out = jax.nn.silu(g) * ret.reshape(B, S, RET_WIDTH)

        mix = jnp.concatenate([conv_out, ret_out], axis=-1) @ w_out[l]
        x = x + gate1 * mix

        h2 = rms_norm(x, norm2_g[l]) * (1.0 + scale2) + shift2
        ff = (jax.nn.silu(h2 @ w_ffn1[l]) * (h2 @ w_ffn3[l])) @ w_ffn2[l]
        x = x + gate2 * ff
    return rms_norm(x, normf_g)
```

```python
import functools
import math

import jax
import jax.numpy as jnp
from jax import lax
from jax.experimental import pallas as pl
from jax.experimental.pallas import tpu as pltpu

F32 = jnp.float32
BF16 = jnp.bfloat16

N_MOD = 6
CONV_WIDTH = 31
RET_HEADS = 8
ROPE_BASE = 10000.0
EPS = 1e-6
LN_EPS = 1e-5

LANES = 128
SUBLANES = 8
VMEM_LIMIT_BYTES = 56 * 1024 * 1024

RET_CHUNK = 256
CONV_HALO = 32


def _params(*semantics):
    return pltpu.CompilerParams(dimension_semantics=semantics, vmem_limit_bytes=VMEM_LIMIT_BYTES)


def _silu(x):
    return x * jax.nn.sigmoid(x)


def _mod_kernel(c_ref, w_ref, b_ref, o_ref):
    ca = _silu(c_ref[...]).astype(BF16)
    o_ref[...] = jnp.dot(ca, w_ref[...].astype(BF16), preferred_element_type=F32) + b_ref[...]


def _modulation(c_pad, w_mod, b_mod, tn=512):
    rows, d = c_pad.shape
    n = w_mod.shape[1]
    return pl.pallas_call(
        _mod_kernel,
        grid=(n // tn,),
        in_specs=[pl.BlockSpec((rows, d), lambda j: (0, 0)),
                  pl.BlockSpec((d, tn), lambda j: (0, j)),
                  pl.BlockSpec((1, tn), lambda j: (0, j))],
        out_specs=pl.BlockSpec((rows, tn), lambda j: (0, j)),
        out_shape=jax.ShapeDtypeStruct((rows, n), F32),
        compiler_params=_params("arbitrary"),
        name="adaln_mod",
    )(c_pad, w_mod, b_mod)


def _norm_mod_kernel(x_ref, g_ref, scale_ref, shift_ref, o_ref):
    x = x_ref[...]
    inv = lax.rsqrt(jnp.mean(x * x, axis=-1, keepdims=True) + EPS)
    y = x * inv * g_ref[...]
    o_ref[...] = (y * (1.0 + scale_ref[...]) + shift_ref[...]).astype(o_ref.dtype)


def _norm_mod(x2, g, mod4, scale_idx, shift_idx, seq, ts=256):
    t, d = x2.shape
    per_b = seq // ts
    return pl.pallas_call(
        _norm_mod_kernel,
        grid=(t // ts,),
        in_specs=[pl.BlockSpec((ts, d), lambda i: (i, 0)),
                  pl.BlockSpec((1, d), lambda i: (0, 0)),
                  pl.BlockSpec((None, None, 1, d), lambda i: (i // per_b, scale_idx, 0, 0)),
                  pl.BlockSpec((None, None, 1, d), lambda i: (i // per_b, shift_idx, 0, 0))],
        out_specs=pl.BlockSpec((ts, d), lambda i: (i, 0)),
        out_shape=jax.ShapeDtypeStruct((t, d), BF16),
        compiler_params=_params("arbitrary"),
        name="rmsnorm_mod",
    )(x2, g, mod4, mod4)


def _matmul_kernel(a_ref, w_ref, o_ref):
    o_ref[...] = jnp.dot(a_ref[...], w_ref[...], preferred_element_type=F32).astype(o_ref.dtype)


def _matmul(a, w, out_dtype, tm=1024, tn=1024):
    m, k = a.shape
    n = w.shape[1]
    return pl.pallas_call(
        _matmul_kernel,
        grid=(m // tm, n // tn),
        in_specs=[pl.BlockSpec((tm, k), lambda i, j: (i, 0)),
                  pl.BlockSpec((k, tn), lambda i, j: (0, j))],
        out_specs=pl.BlockSpec((tm, tn), lambda i, j: (i, j)),
        out_shape=jax.ShapeDtypeStruct((m, n), out_dtype),
        compiler_params=_params("arbitrary", "arbitrary"),
        name="in_proj",
    )(a, w)


def _conv_kernel(val_ref, gate_ref, w_ref, cb_ref, lg_ref, lb_ref, o_ref, a_scr, y_scr, *, ts, ch):
    s = pl.program_id(1)

    @pl.when(s == 0)
    def _():
        a_scr[0:CONV_HALO, :] = jnp.zeros((CONV_HALO, ch), F32)

    @pl.when(s > 0)
    def _():
        a_scr[0:CONV_HALO, :] = a_scr[ts:ts + CONV_HALO, :]

    a_scr[CONV_HALO:CONV_HALO + ts, :] = (
        val_ref[...].astype(F32) * jax.nn.sigmoid(gate_ref[...].astype(F32)))

    rows = 32
    first = CONV_HALO - (CONV_WIDTH - 1)

    def chunk_body(c, carry):
        c0 = pl.multiple_of(c * LANES, LANES)
        lanes = pl.ds(c0, LANES)
        for r in range(ts // rows):
            acc = jnp.zeros((rows, LANES), F32)
            for j in range(CONV_WIDTH):
                acc = acc + w_ref[j:j + 1, lanes] * a_scr[pl.ds(r * rows + first + j, rows), lanes]
            y_scr[r * rows:(r + 1) * rows, lanes] = acc + cb_ref[:, lanes]
        return carry

    lax.fori_loop(0, ch // LANES, chunk_body, 0)

    ln_rows = 16
    for r in range(ts // ln_rows):
        y = y_scr[r * ln_rows:(r + 1) * ln_rows, :]
        mu = jnp.mean(y, axis=-1, keepdims=True)
        d = y - mu
        var = jnp.mean(d * d, axis=-1, keepdims=True)
        z = d * lax.rsqrt(var + LN_EPS) * lg_ref[...] + lb_ref[...]
        o_ref[r * ln_rows:(r + 1) * ln_rows, :] = _silu(z).astype(o_ref.dtype)


def _conv_group(p, dw_w, dw_b, ln_g, ln_b, batch, seq, ch, ts=256):
    t = p.shape[0]
    per_b = seq // ts
    kern = functools.partial(_conv_kernel, ts=ts, ch=ch)
    vec = lambda: pl.BlockSpec((1, ch), lambda b, s: (0, 0))
    return pl.pallas_call(
        kern,
        grid=(batch, per_b),
        in_specs=[pl.BlockSpec((ts, ch), lambda b, s: (b * per_b + s, 0)),
                  pl.BlockSpec((ts, ch), lambda b, s: (b * per_b + s, 1)),
                  pl.BlockSpec((CONV_WIDTH, ch), lambda b, s: (0, 0)),
                  vec(), vec(), vec()],
        out_specs=pl.BlockSpec((ts, ch), lambda b, s: (b * per_b + s, 0)),
        out_shape=jax.ShapeDtypeStruct((t, ch), BF16),
        scratch_shapes=[pltpu.VMEM((CONV_HALO + ts, ch), F32),
                        pltpu.VMEM((ts, ch), F32)],
        compiler_params=_params("arbitrary", "arbitrary"),
        name="conv_group",
    )(p, p, dw_w, dw_b, ln_g, ln_b)


def _rope_table_kernel(pos_ref, freq_ref, cos_ref, sin_ref):
    ang = pos_ref[...] * freq_ref[...]
    cos_ref[...] = jnp.cos(ang)
    sin_ref[...] = jnp.sin(ang)


def _rope_tables(pos_b, freqs, ts=512):
    t, half = pos_b.shape
    blk = lambda: pl.BlockSpec((ts, half), lambda i: (i, 0))
    return pl.pallas_call(
        _rope_table_kernel,
        grid=(t // ts,),
        in_specs=[blk(), pl.BlockSpec((1, half), lambda i: (0, 0))],
        out_specs=[blk(), blk()],
        out_shape=[jax.ShapeDtypeStruct((t, half), F32)] * 2,
        compiler_params=_params("arbitrary"),
        name="rope_tables",
    )(pos_b, freqs)


def _ret_kernel(q_ref, k_ref, v_ref, g_ref, cos_ref, sin_ref, dec_ref, xi_ref, zeta_ref, cd_ref,
                gn_ref, o_ref, state_ref, *, half, kscale):
    @pl.when(pl.program_id(2) == 0)
    def _():
        state_ref[...] = jnp.zeros(state_ref.shape, F32)

    cos = cos_ref[...]
    sin = sin_ref[...]

    def rope(t_ref):
        t1 = t_ref[:, :half].astype(F32)
        t2 = t_ref[:, half:].astype(F32)
        return jnp.concatenate([t1 * cos - t2 * sin, t2 * cos + t1 * sin], axis=-1)

    qr = rope(q_ref)
    kr = rope(k_ref) * kscale
    v = v_ref[...]

    scores = lax.dot_general(qr.astype(BF16), kr.astype(BF16), (((1,), (1,)), ((), ())),
                             preferred_element_type=F32) * dec_ref[...]
    inner = jnp.dot(scores.astype(BF16), v, preferred_element_type=F32)

    state = state_ref[...]
    cross = jnp.dot((qr * xi_ref[...]).astype(BF16), state.astype(BF16),
                    preferred_element_type=F32)
    kz_t = jnp.transpose(kr * zeta_ref[...]).astype(BF16)
    state_ref[...] = state * cd_ref[...] + jnp.dot(kz_t, v, preferred_element_type=F32)

    out = inner + cross
    mu = jnp.mean(out, axis=-1, keepdims=True)
    d = out - mu
    var = jnp.mean(d * d, axis=-1, keepdims=True)
    y = d * lax.rsqrt(var + LN_EPS) * gn_ref[...]
    o_ref[...] = (_silu(g_ref[...].astype(F32)) * y).astype(o_ref.dtype)


def _retention_group(p, cos_t, sin_t, dec, xi_b, zeta_b, cd_b, gn_g, batch, seq, col0, hd):
    t = p.shape[0]
    c = RET_CHUNK
    nc = seq // c
    h = RET_HEADS
    qb, kb, vb, gb = (col0 // hd + i * h for i in range(4))
    row = lambda b, hh, n: b * nc + n
    head_blk = lambda off: pl.BlockSpec((c, hd), lambda b, hh, n: (row(b, hh, n), off + hh))
    tab = lambda w: pl.BlockSpec((None, c, w), lambda b, hh, n: (hh, 0, 0))
    kern = functools.partial(_ret_kernel, half=hd // 2, kscale=float(hd) ** -0.5)
    return pl.pallas_call(
        kern,
        grid=(batch, h, nc),
        in_specs=[head_blk(qb), head_blk(kb), head_blk(vb), head_blk(gb),
                  pl.BlockSpec((c, hd // 2), lambda b, hh, n: (row(b, hh, n), 0)),
                  pl.BlockSpec((c, hd // 2), lambda b, hh, n: (row(b, hh, n), 0)),
                  tab(c), tab(hd), tab(hd),
                  pl.BlockSpec((None, 1, hd), lambda b, hh, n: (hh, 0, 0)),
                  pl.BlockSpec((1, hd), lambda b, hh, n: (0, hh))],
        out_specs=pl.BlockSpec((c, hd), lambda b, hh, n: (row(b, hh, n), hh)),
        out_shape=jax.ShapeDtypeStruct((t, h * hd), BF16),
        scratch_shapes=[pltpu.VMEM((hd, hd), F32)],
        compiler_params=_params("arbitrary", "arbitrary", "arbitrary"),
        name="retention_group",
    )(p, p, p, p, cos_t, sin_t, dec, xi_b, zeta_b, cd_b, gn_g)


def _retention_tables(hd):
    c = RET_CHUNK
    heads = jnp.arange(RET_HEADS, dtype=F32)
    log_gamma = jnp.log(1.0 - 2.0 ** (-5.0 - heads))
    idx = jnp.arange(c)
    idx_f = idx.astype(F32)
    diff = (idx[:, None] - idx[None, :]).astype(F32)
    dec = jnp.where(diff[None] >= 0,
                    jnp.exp(jnp.maximum(diff, 0.0)[None] * log_gamma[:, None, None]), 0.0)
    xi = jnp.exp((idx_f + 1.0)[None] * log_gamma[:, None])
    zeta = jnp.exp((c - 1.0 - idx_f)[None] * log_gamma[:, None])
    cd = jnp.exp(c * log_gamma)
    bcast = lambda a: jnp.broadcast_to(a[:, :, None], (RET_HEADS, c, hd))
    return dec, bcast(xi), bcast(zeta), jnp.broadcast_to(cd[:, None, None], (RET_HEADS, 1, hd))


def _outproj_kernel(a1_ref, a2_ref, w1_ref, w2_ref, x_ref, gate_ref, o_ref):
    mix = (jnp.dot(a1_ref[...], w1_ref[...], preferred_element_type=F32)
           + jnp.dot(a2_ref[...], w2_ref[...], preferred_element_type=F32))
    o_ref[...] = x_ref[...] + gate_ref[...] * mix


def _outproj(a1, a2, w, x2, mod4, gate_idx, seq, tm=1024, tn=512):
    m, k1 = a1.shape
    k2 = a2.shape[1]
    n = w.shape[1]
    per_b = seq // tm
    return pl.pallas_call(
        _outproj_kernel,
        grid=(m // tm, n // tn),
        in_specs=[pl.BlockSpec((tm, k1), lambda i, j: (i, 0)),
                  pl.BlockSpec((tm, k2), lambda i, j: (i, 0)),
                  pl.BlockSpec((k1, tn), lambda i, j: (0, j)),
                  pl.BlockSpec((k2, tn), lambda i, j: (k1 // k2, j)),
                  pl.BlockSpec((tm, tn), lambda i, j: (i, j)),
                  pl.BlockSpec((None, None, 1, tn), lambda i, j: (i // per_b, gate_idx, 0, j))],
        out_specs=pl.BlockSpec((tm, tn), lambda i, j: (i, j)),
        out_shape=jax.ShapeDtypeStruct((m, n), F32),
        compiler_params=_params("arbitrary", "arbitrary"),
        name="out_proj",
    )(a1, a2, w, w, x2, mod4)


def _ffn_kernel(h_ref, w1_ref, w3_ref, w2_ref, o_ref, acc_ref):
    f = pl.program_id(1)

    @pl.when(f == 0)
    def _():
        acc_ref[...] = jnp.zeros(acc_ref.shape, F32)

    h = h_ref[...]
    a = jnp.dot(h, w1_ref[...], preferred_element_type=F32)
    b = jnp.dot(h, w3_ref[...], preferred_element_type=F32)
    u = (_silu(a) * b).astype(BF16)
    acc_ref[...] += jnp.dot(u, w2_ref[...], preferred_element_type=F32)

    @pl.when(f == pl.num_programs(1) - 1)
    def _():
        o_ref[...] = acc_ref[...].astype(o_ref.dtype)


def _ffn(h2, w1, w3, w2, tm=512, tf=256):
    m, d = h2.shape
    f = w1.shape[1]
    return pl.pallas_call(
        _ffn_kernel,
        grid=(m // tm, f // tf),
        in_specs=[pl.BlockSpec((tm, d), lambda i, j: (i, 0)),
                  pl.BlockSpec((d, tf), lambda i, j: (0, j)),
                  pl.BlockSpec((d, tf), lambda i, j: (0, j)),
                  pl.BlockSpec((tf, d), lambda i, j: (j, 0))],
        out_specs=pl.BlockSpec((tm, d), lambda i, j: (i, 0)),
        out_shape=jax.ShapeDtypeStruct((m, d), BF16),
        scratch_shapes=[pltpu.VMEM((tm, d), F32)],
        compiler_params=_params("arbitrary", "arbitrary"),
        name="swiglu_ffn",
    )(h2, w1, w3, w2)


def _residual_kernel(x_ref, ff_ref, gate_ref, g_ref, o_ref, *, final_norm):
    y = x_ref[...] + gate_ref[...] * ff_ref[...].astype(F32)
    if final_norm:
        inv = lax.rsqrt(jnp.mean(y * y, axis=-1, keepdims=True) + EPS)
        y = y * inv * g_ref[...]
    o_ref[...] = y


def _residual(x1, ff, mod4, gate_idx, g, seq, final_norm, ts=256):
    t, d = x1.shape
    per_b = seq // ts
    return pl.pallas_call(
        functools.partial(_residual_kernel, final_norm=final_norm),
        grid=(t // ts,),
        in_specs=[pl.BlockSpec((ts, d), lambda i: (i, 0)),
                  pl.BlockSpec((ts, d), lambda i: (i, 0)),
                  pl.BlockSpec((None, None, 1, d), lambda i: (i // per_b, gate_idx, 0, 0)),
                  pl.BlockSpec((1, d), lambda i: (0, 0))],
        out_specs=pl.BlockSpec((ts, d), lambda i: (i, 0)),
        out_shape=jax.ShapeDtypeStruct((t, d), F32),
        compiler_params=_params("arbitrary"),
        name="final_norm",
    )(x1, ff, mod4, g)


def kernel(x, c, positions, norm1_g, norm2_g, normf_g, w_mod, b_mod, w_in, conv_dw_w, conv_dw_b,
           conv_ln_g, conv_ln_b, ret_gn_g, w_out, w_ffn1, w_ffn3, w_ffn2):
    batch, seq, d = x.shape
    depth = w_mod.shape[0]
    t = batch * seq
    conv_ch = conv_dw_w.shape[-1]
    ret_w = ret_gn_g.shape[-1]
    hd = ret_w // RET_HEADS
    off_q = 2 * conv_ch

    half = hd // 2
    freqs = (ROPE_BASE ** (-jnp.arange(half, dtype=F32) / half)).reshape(1, half)
    pos_b = jnp.broadcast_to(positions.astype(F32).reshape(t, 1), (t, half))
    cos_t, sin_t = _rope_tables(pos_b, freqs)
    dec, xi_b, zeta_b, cd_b = _retention_tables(hd)

    c_pad = jnp.pad(c, ((0, SUBLANES - batch % SUBLANES), (0, 0))) if batch % SUBLANES else c
    x2 = x.reshape(t, d)
    row = lambda v: v.reshape(1, -1)

    for l in range(depth):
        mod = _modulation(c_pad, w_mod[l], row(b_mod[l]))
        mod4 = mod[:batch].reshape(batch, N_MOD, 1, d)

        h = _norm_mod(x2, row(norm1_g[l]), mod4, 1, 0, seq)
        p = _matmul(h, w_in[l].astype(BF16), BF16)

        conv_out = _conv_group(p, conv_dw_w[l], row(conv_dw_b[l]), row(conv_ln_g[l]),
                               row(conv_ln_b[l]), batch, seq, conv_ch)
        ret_out = _retention_group(p, cos_t, sin_t, dec, xi_b, zeta_b, cd_b, row(ret_gn_g[l]),
                                   batch, seq, off_q, hd)

        x2 = _outproj(conv_out, ret_out, w_out[l].astype(BF16), x2, mod4, 2, seq)

        h2 = _norm_mod(x2, row(norm2_g[l]), mod4, 4, 3, seq)
        ff = _ffn(h2, w_ffn1[l].astype(BF16), w_ffn3[l].astype(BF16), w_ffn2[l].astype(BF16))
        x2 = _residual(x2, ff, mod4, 5, row(normf_g), seq, final_norm=(l + 1 == depth))

    return x2.reshape(batch, seq, d)
```

```python
import functools
import math

import jax
import jax.numpy as jnp
from jax import lax
from jax.experimental import pallas as pl
from jax.experimental.pallas import tpu as pltpu

F32 = jnp.float32
BF16 = jnp.bfloat16

N_MOD = 6
CONV_WIDTH = 31
RET_HEADS = 8
ROPE_BASE = 10000.0
EPS = 1e-6
LN_EPS = 1e-5

LANES = 128
SUBLANES = 8
VMEM_LIMIT_BYTES = 56 * 1024 * 1024

RET_CHUNK = 256
CONV_HALO = 32


def _params(*semantics):
    return pltpu.CompilerParams(dimension_semantics=semantics, vmem_limit_bytes=VMEM_LIMIT_BYTES)


def _silu(x):
    return x * jax.nn.sigmoid(x)


def _mod_kernel(c_ref, w_ref, b_ref, o_ref):
    ca = _silu(c_ref[...]).astype(BF16)
    o_ref[...] = jnp.dot(ca, w_ref[...].astype(BF16), preferred_element_type=F32) + b_ref[...]


def _modulation(c_pad, w_mod, b_mod, tn=512):
    rows, d = c_pad.shape
    n = w_mod.shape[1]
    return pl.pallas_call(
        _mod_kernel,
        grid=(n // tn,),
        in_specs=[pl.BlockSpec((rows, d), lambda j: (0, 0)),
                  pl.BlockSpec((d, tn), lambda j: (0, j)),
                  pl.BlockSpec((1, tn), lambda j: (0, j))],
        out_specs=pl.BlockSpec((rows, tn), lambda j: (0, j)),
        out_shape=jax.ShapeDtypeStruct((rows, n), F32),
        compiler_params=_params("arbitrary"),
        name="adaln_mod",
    )(c_pad, w_mod, b_mod)


def _norm_mod_kernel(x_ref, g_ref, scale_ref, shift_ref, o_ref):
    x = x_ref[...]
    inv = lax.rsqrt(jnp.mean(x * x, axis=-1, keepdims=True) + EPS)
    y = x * inv * g_ref[...]
    o_ref[...] = (y * (1.0 + scale_ref[...]) + shift_ref[...]).astype(o_ref.dtype)


def _norm_mod(x2, g, mod4, scale_idx, shift_idx, seq, ts=256):
    t, d = x2.shape
    per_b = seq // ts
    return pl.pallas_call(
        _norm_mod_kernel,
        grid=(t // ts,),
        in_specs=[pl.BlockSpec((ts, d), lambda i: (i, 0)),
                  pl.BlockSpec((1, d), lambda i: (0, 0)),
                  pl.BlockSpec((None, None, 1, d), lambda i: (i // per_b, scale_idx, 0, 0)),
                  pl.BlockSpec((None, None, 1, d), lambda i: (i // per_b, shift_idx, 0, 0))],
        out_specs=pl.BlockSpec((ts, d), lambda i: (i, 0)),
        out_shape=jax.ShapeDtypeStruct((t, d), BF16),
        compiler_params=_params("arbitrary"),
        name="rmsnorm_mod",
    )(x2, g, mod4, mod4)


def _matmul_kernel(a_ref, w_ref, o_ref):
    o_ref[...] = jnp.dot(a_ref[...], w_ref[...], preferred_element_type=F32).astype(o_ref.dtype)


def _matmul(a, w, out_dtype, tm=1024, tn=1024):
    m, k = a.shape
    n = w.shape[1]
    return pl.pallas_call(
        _matmul_kernel,
        grid=(m // tm, n // tn),
        in_specs=[pl.BlockSpec((tm, k), lambda i, j: (i, 0)),
                  pl.BlockSpec((k, tn), lambda i, j: (0, j))],
        out_specs=pl.BlockSpec((tm, tn), lambda i, j: (i, j)),
        out_shape=jax.ShapeDtypeStruct((m, n), out_dtype),
        compiler_params=_params("arbitrary", "arbitrary"),
        name="in_proj",
    )(a, w)


def _conv_kernel(val_ref, gate_ref, w_ref, cb_ref, lg_ref, lb_ref, o_ref, a_scr, y_scr, *, ts, ch):
    s = pl.program_id(1)

    @pl.when(s == 0)
    def _():
        a_scr[0:CONV_HALO, :] = jnp.zeros((CONV_HALO, ch), F32)

    @pl.when(s > 0)
    def _():
        a_scr[0:CONV_HALO, :] = a_scr[ts:ts + CONV_HALO, :]

    a_scr[CONV_HALO:CONV_HALO + ts, :] = (
        val_ref[...].astype(F32) * jax.nn.sigmoid(gate_ref[...].astype(F32)))

    rows = 32
    first = CONV_HALO - (CONV_WIDTH - 1)

    def chunk_body(c, carry):
        c0 = pl.multiple_of(c * LANES, LANES)
        lanes = pl.ds(c0, LANES)
        for r in range(ts // rows):
            acc = jnp.zeros((rows, LANES), F32)
            for j in range(CONV_WIDTH):
                acc = acc + w_ref[j:j + 1, lanes] * a_scr[pl.ds(r * rows + first + j, rows), lanes]
            y_scr[r * rows:(r + 1) * rows, lanes] = acc + cb_ref[:, lanes]
        return carry

    lax.fori_loop(0, ch // LANES, chunk_body, 0)

    ln_rows = 16
    for r in range(ts // ln_rows):
        y = y_scr[r * ln_rows:(r + 1) * ln_rows, :]
        mu = jnp.mean(y, axis=-1, keepdims=True)
        d = y - mu
        var = jnp.mean(d * d, axis=-1, keepdims=True)
        z = d * lax.rsqrt(var + LN_EPS) * lg_ref[...] + lb_ref[...]
        o_ref[r * ln_rows:(r + 1) * ln_rows, :] = _silu(z).astype(o_ref.dtype)


def _conv_group(p, dw_w, dw_b, ln_g, ln_b, batch, seq, ch, ts=256):
    t = p.shape[0]
    per_b = seq // ts
    kern = functools.partial(_conv_kernel, ts=ts, ch=ch)
    vec = lambda: pl.BlockSpec((1, ch), lambda b, s: (0, 0))
    return pl.pallas_call(
        kern,
        grid=(batch, per_b),
        in_specs=[pl.BlockSpec((ts, ch), lambda b, s: (b * per_b + s, 0)),
                  pl.BlockSpec((ts, ch), lambda b, s: (b * per_b + s, 1)),
                  pl.BlockSpec((CONV_WIDTH, ch), lambda b, s: (0, 0)),
                  vec(), vec(), vec()],
        out_specs=pl.BlockSpec((ts, ch), lambda b, s: (b * per_b + s, 0)),
        out_shape=jax.ShapeDtypeStruct((t, ch), BF16),
        scratch_shapes=[pltpu.VMEM((CONV_HALO + ts, ch), F32),
                        pltpu.VMEM((ts, ch), F32)],
        compiler_params=_params("arbitrary", "arbitrary"),
        name="conv_group",
    )(p, p, dw_w, dw_b, ln_g, ln_b)


def _rope_table_kernel(pos_ref, freq_ref, cos_ref, sin_ref):
    ang = pos_ref[...] * freq_ref[...]
    cos_ref[...] = jnp.cos(ang)
    sin_ref[...] = jnp.sin(ang)


def _rope_tables(pos_b, freqs, ts=512):
    t, half = pos_b.shape
    blk = lambda: pl.BlockSpec((ts, half), lambda i: (i, 0))
    return pl.pallas_call(
        _rope_table_kernel,
        grid=(t // ts,),
        in_specs=[blk(), pl.BlockSpec((1, half), lambda i: (0, 0))],
        out_specs=[blk(), blk()],
        out_shape=[jax.ShapeDtypeStruct((t, half), F32)] * 2,
        compiler_params=_params("arbitrary"),
        name="rope_tables",
    )(pos_b, freqs)


def _ret_kernel(q_ref, k_ref, v_ref, g_ref, cos_ref, sin_ref, dec_ref, xi_ref, zeta_ref, cd_ref,
                gn_ref, o_ref, state_ref, *, heads, hd, kscale):
    @pl.when(pl.program_id(1) == 0)
    def _():
        state_ref[...] = jnp.zeros(state_ref.shape, F32)

    half = hd // 2
    cos = cos_ref[...]
    sin = sin_ref[...]

    def rope(t_ref, lo):
        t1 = t_ref[:, lo:lo + half].astype(F32)
        t2 = t_ref[:, lo + half:lo + hd].astype(F32)
        return jnp.concatenate([t1 * cos - t2 * sin, t2 * cos + t1 * sin], axis=-1)

    for hh in range(heads):
        lo = hh * hd
        cols = slice(lo, lo + hd)
        qr = rope(q_ref, lo)
        kr = rope(k_ref, lo) * kscale
        v = v_ref[:, cols]

        scores = lax.dot_general(qr.astype(BF16), kr.astype(BF16), (((1,), (1,)), ((), ())),
                                 preferred_element_type=F32) * dec_ref[hh]
        inner = jnp.dot(scores.astype(BF16), v, preferred_element_type=F32)

        state = state_ref[hh]
        cross = jnp.dot((qr * xi_ref[hh]).astype(BF16), state.astype(BF16),
                        preferred_element_type=F32)
        kz_t = jnp.transpose(kr * zeta_ref[hh]).astype(BF16)
        state_ref[hh] = state * cd_ref[hh] + jnp.dot(kz_t, v, preferred_element_type=F32)

        out = inner + cross
        mu = jnp.mean(out, axis=-1, keepdims=True)
        d = out - mu
        var = jnp.mean(d * d, axis=-1, keepdims=True)
        y = d * lax.rsqrt(var + LN_EPS) * gn_ref[:, cols]
        o_ref[:, cols] = (_silu(g_ref[:, cols].astype(F32)) * y).astype(o_ref.dtype)


def _retention_group(p, cos_t, sin_t, dec, xi_b, zeta_b, cd_b, gn_g, batch, seq, col0, hd):
    t = p.shape[0]
    c = RET_CHUNK
    nc = seq // c
    h = RET_HEADS
    w = h * hd
    first = col0 // w
    row = lambda b, n: b * nc + n
    group_blk = lambda i: pl.BlockSpec((c, w), lambda b, n: (row(b, n), first + i))
    whole = lambda a: pl.BlockSpec(a.shape, lambda b, n: (0,) * a.ndim)
    rope_blk = lambda: pl.BlockSpec((c, hd // 2), lambda b, n: (row(b, n), 0))
    kern = functools.partial(_ret_kernel, heads=h, hd=hd, kscale=float(hd) ** -0.5)
    return pl.pallas_call(
        kern,
        grid=(batch, nc),
        in_specs=[group_blk(0), group_blk(1), group_blk(2), group_blk(3),
                  rope_blk(), rope_blk(),
                  whole(dec), whole(xi_b), whole(zeta_b), whole(cd_b), whole(gn_g)],
        out_specs=pl.BlockSpec((c, w), lambda b, n: (row(b, n), 0)),
        out_shape=jax.ShapeDtypeStruct((t, w), BF16),
        scratch_shapes=[pltpu.VMEM((h, hd, hd), F32)],
        compiler_params=_params("arbitrary", "arbitrary"),
        name="retention_group",
    )(p, p, p, p, cos_t, sin_t, dec, xi_b, zeta_b, cd_b, gn_g)


def _retention_tables(hd):
    c = RET_CHUNK
    heads = jnp.arange(RET_HEADS, dtype=F32)
    log_gamma = jnp.log(1.0 - 2.0 ** (-5.0 - heads))
    idx = jnp.arange(c)
    idx_f = idx.astype(F32)
    diff = (idx[:, None] - idx[None, :]).astype(F32)
    dec = jnp.where(diff[None] >= 0,
                    jnp.exp(jnp.maximum(diff, 0.0)[None] * log_gamma[:, None, None]), 0.0)
    xi = jnp.exp((idx_f + 1.0)[None] * log_gamma[:, None])
    zeta = jnp.exp((c - 1.0 - idx_f)[None] * log_gamma[:, None])
    cd = jnp.exp(c * log_gamma)
    bcast = lambda a: jnp.broadcast_to(a[:, :, None], (RET_HEADS, c, hd))
    return dec, bcast(xi), bcast(zeta), jnp.broadcast_to(cd[:, None, None], (RET_HEADS, 1, hd))


def _outproj_kernel(a1_ref, a2_ref, w1_ref, w2_ref, x_ref, gate_ref, o_ref):
    mix = (jnp.dot(a1_ref[...], w1_ref[...], preferred_element_type=F32)
           + jnp.dot(a2_ref[...], w2_ref[...], preferred_element_type=F32))
    o_ref[...] = x_ref[...] + gate_ref[...] * mix


def _outproj(a1, a2, w, x2, mod4, gate_idx, seq, tm=1024, tn=512):
    m, k1 = a1.shape
    k2 = a2.shape[1]
    n = w.shape[1]
    per_b = seq // tm
    return pl.pallas_call(
        _outproj_kernel,
        grid=(m // tm, n // tn),
        in_specs=[pl.BlockSpec((tm, k1), lambda i, j: (i, 0)),
                  pl.BlockSpec((tm, k2), lambda i, j: (i, 0)),
                  pl.BlockSpec((k1, tn), lambda i, j: (0, j)),
                  pl.BlockSpec((k2, tn), lambda i, j: (k1 // k2, j)),
                  pl.BlockSpec((tm, tn), lambda i, j: (i, j)),
                  pl.BlockSpec((None, None, 1, tn), lambda i, j: (i // per_b, gate_idx, 0, j))],
        out_specs=pl.BlockSpec((tm, tn), lambda i, j: (i, j)),
        out_shape=jax.ShapeDtypeStruct((m, n), F32),
        compiler_params=_params("arbitrary", "arbitrary"),
        name="out_proj",
    )(a1, a2, w, w, x2, mod4)


def _ffn_kernel(h_ref, w1_ref, w3_ref, w2_ref, o_ref, acc_ref):
    f = pl.program_id(1)

    @pl.when(f == 0)
    def _():
        acc_ref[...] = jnp.zeros(acc_ref.shape, F32)

    h = h_ref[...]
    a = jnp.dot(h, w1_ref[...], preferred_element_type=F32)
    b = jnp.dot(h, w3_ref[...], preferred_element_type=F32)
    u = (_silu(a) * b).astype(BF16)
    acc_ref[...] += jnp.dot(u, w2_ref[...], preferred_element_type=F32)

    @pl.when(f == pl.num_programs(1) - 1)
    def _():
        o_ref[...] = acc_ref[...].astype(o_ref.dtype)


def _ffn(h2, w1, w3, w2, tm=1024, tf=256):
    m, d = h2.shape
    f = w1.shape[1]
    once = pl.Buffered(1)
    return pl.pallas_call(
        _ffn_kernel,
        grid=(m // tm, f // tf),
        in_specs=[pl.BlockSpec((tm, d), lambda i, j: (i, 0), pipeline_mode=once),
                  pl.BlockSpec((d, tf), lambda i, j: (0, j)),
                  pl.BlockSpec((d, tf), lambda i, j: (0, j)),
                  pl.BlockSpec((tf, d), lambda i, j: (j, 0))],
        out_specs=pl.BlockSpec((tm, d), lambda i, j: (i, 0), pipeline_mode=once),
        out_shape=jax.ShapeDtypeStruct((m, d), BF16),
        scratch_shapes=[pltpu.VMEM((tm, d), F32)],
        compiler_params=_params("arbitrary", "arbitrary"),
        name="swiglu_ffn",
    )(h2, w1, w3, w2)


def _residual_kernel(x_ref, ff_ref, gate_ref, g_ref, o_ref, *, final_norm):
    y = x_ref[...] + gate_ref[...] * ff_ref[...].astype(F32)
    if final_norm:
        inv = lax.rsqrt(jnp.mean(y * y, axis=-1, keepdims=True) + EPS)
        y = y * inv * g_ref[...]
    o_ref[...] = y


def _residual(x1, ff, mod4, gate_idx, g, seq, final_norm, ts=256):
    t, d = x1.shape
    per_b = seq // ts
    return pl.pallas_call(
        functools.partial(_residual_kernel, final_norm=final_norm),
        grid=(t // ts,),
        in_specs=[pl.BlockSpec((ts, d), lambda i: (i, 0)),
                  pl.BlockSpec((ts, d), lambda i: (i, 0)),
                  pl.BlockSpec((None, None, 1, d), lambda i: (i // per_b, gate_idx, 0, 0)),
                  pl.BlockSpec((1, d), lambda i: (0, 0))],
        out_specs=pl.BlockSpec((ts, d), lambda i: (i, 0)),
        out_shape=jax.ShapeDtypeStruct((t, d), F32),
        compiler_params=_params("arbitrary"),
        name="final_norm",
    )(x1, ff, mod4, g)


def kernel(x, c, positions, norm1_g, norm2_g, normf_g, w_mod, b_mod, w_in, conv_dw_w, conv_dw_b,
           conv_ln_g, conv_ln_b, ret_gn_g, w_out, w_ffn1, w_ffn3, w_ffn2):
    batch, seq, d = x.shape
    depth = w_mod.shape[0]
    t = batch * seq
    conv_ch = conv_dw_w.shape[-1]
    ret_w = ret_gn_g.shape[-1]
    hd = ret_w // RET_HEADS
    off_q = 2 * conv_ch

    half = hd // 2
    freqs = (ROPE_BASE ** (-jnp.arange(half, dtype=F32) / half)).reshape(1, half)
    pos_b = jnp.broadcast_to(positions.astype(F32).reshape(t, 1), (t, half))
    cos_t, sin_t = _rope_tables(pos_b, freqs)
    dec, xi_b, zeta_b, cd_b = _retention_tables(hd)

    c_pad = jnp.pad(c, ((0, SUBLANES - batch % SUBLANES), (0, 0))) if batch % SUBLANES else c
    x2 = x.reshape(t, d)
    row = lambda v: v.reshape(1, -1)

    for l in range(depth):
        mod = _modulation(c_pad, w_mod[l], row(b_mod[l]))
        mod4 = mod[:batch].reshape(batch, N_MOD, 1, d)

        h = _norm_mod(x2, row(norm1_g[l]), mod4, 1, 0, seq)
        p = _matmul(h, w_in[l].astype(BF16), BF16)

        conv_out = _conv_group(p, conv_dw_w[l], row(conv_dw_b[l]), row(conv_ln_g[l]),
                               row(conv_ln_b[l]), batch, seq, conv_ch)
        ret_out = _retention_group(p, cos_t, sin_t, dec, xi_b, zeta_b, cd_b, row(ret_gn_g[l]),
                                   batch, seq, off_q, hd)

        x2 = _outproj(conv_out, ret_out, w_out[l].astype(BF16), x2, mod4, 2, seq)

        h2 = _norm_mod(x2, row(norm2_g[l]), mod4, 4, 3, seq)
        ff = _ffn(h2, w_ffn1[l].astype(BF16), w_ffn3[l].astype(BF16), w_ffn2[l].astype(BF16))
        x2 = _residual(x2, ff, mod4, 5, row(normf_g), seq, final_norm=(l + 1 == depth))

    return x2.reshape(batch, seq, d)
```

```python
import functools
import math

import jax
import jax.numpy as jnp
from jax import lax
from jax.experimental import pallas as pl
from jax.experimental.pallas import tpu as pltpu

F32 = jnp.float32
BF16 = jnp.bfloat16

N_MOD = 6
CONV_WIDTH = 31
RET_HEADS = 8
ROPE_BASE = 10000.0
EPS = 1e-6
LN_EPS = 1e-5

LANES = 128
SUBLANES = 8
VMEM_LIMIT_BYTES = 56 * 1024 * 1024

RET_CHUNK = 256
CONV_HALO = 32
CONV_TILE_ROWS = 32
LN_ROWS = 16
IN_PROJ_TM = 1024
IN_PROJ_TN = 1024


def _params(*semantics, flags=None):
    return pltpu.CompilerParams(dimension_semantics=semantics, vmem_limit_bytes=VMEM_LIMIT_BYTES,
                                flags=flags)


def _silu(x):
    return x * jax.nn.sigmoid(x)


def _mod_kernel(c_ref, w_ref, b_ref, o_ref):
    ca = _silu(c_ref[...]).astype(BF16)
    o_ref[...] = jnp.dot(ca, w_ref[...].astype(BF16), preferred_element_type=F32) + b_ref[...]


def _modulation(c_pad, w_mod, b_mod, tn=512):
    rows, d = c_pad.shape
    n = w_mod.shape[1]
    return pl.pallas_call(
        _mod_kernel,
        grid=(n // tn,),
        in_specs=[pl.BlockSpec((rows, d), lambda j: (0, 0)),
                  pl.BlockSpec((d, tn), lambda j: (0, j)),
                  pl.BlockSpec((1, tn), lambda j: (0, j))],
        out_specs=pl.BlockSpec((rows, tn), lambda j: (0, j)),
        out_shape=jax.ShapeDtypeStruct((rows, n), F32),
        compiler_params=_params("arbitrary"),
        name="adaln_mod",
    )(c_pad, w_mod, b_mod)


def _norm_mod_kernel(x_ref, g_ref, scale_ref, shift_ref, o_ref):
    x = x_ref[...]
    inv = lax.rsqrt(jnp.mean(x * x, axis=-1, keepdims=True) + EPS)
    y = x * inv * g_ref[...]
    o_ref[...] = (y * (1.0 + scale_ref[...]) + shift_ref[...]).astype(o_ref.dtype)


def _norm_mod(x2, g, mod4, scale_idx, shift_idx, seq, ts=256):
    t, d = x2.shape
    per_b = seq // ts
    return pl.pallas_call(
        _norm_mod_kernel,
        grid=(t // ts,),
        in_specs=[pl.BlockSpec((ts, d), lambda i: (i, 0)),
                  pl.BlockSpec((1, d), lambda i: (0, 0)),
                  pl.BlockSpec((None, None, 1, d), lambda i: (i // per_b, scale_idx, 0, 0)),
                  pl.BlockSpec((None, None, 1, d), lambda i: (i // per_b, shift_idx, 0, 0))],
        out_specs=pl.BlockSpec((ts, d), lambda i: (i, 0)),
        out_shape=jax.ShapeDtypeStruct((t, d), BF16),
        compiler_params=_params("arbitrary"),
        name="rmsnorm_mod",
    )(x2, g, mod4, mod4)


def _zero_after(x):
    s = jnp.sum(x[0:SUBLANES, :], axis=1, keepdims=True)
    bits = lax.bitcast_convert_type(jnp.broadcast_to(s[0:1, :], (1, LANES)), jnp.uint32)
    sixteen = jnp.uint32(16)
    bits = lax.shift_right_logical(lax.shift_right_logical(bits, sixteen), sixteen)
    return lax.bitcast_convert_type(bits, F32)


def _conv_tile(win_ref, cw_ref, base, lanes, after=None):
    first = CONV_HALO - (CONV_WIDTH - 1)
    y = None
    for r in range(SUBLANES):
        off = first + r
        lo = base + (off // SUBLANES) * SUBLANES
        sub = off % SUBLANES
        span = CONV_TILE_ROWS + (SUBLANES if sub else 0)
        u = None
        for q in range((CONV_WIDTH - 1 - r) // SUBLANES + 1):
            tap = q * SUBLANES + r
            rows = slice(lo + q * SUBLANES, lo + q * SUBLANES + span)
            w_row = cw_ref[tap:tap + 1, lanes]
            if after is not None and tap == 0:
                w_row = w_row + after
            term = w_row * win_ref[rows, lanes]
            u = term if u is None else u + term
        u = u[sub:sub + CONV_TILE_ROWS, :]
        y = u if y is None else y + u
    return y


def _inproj_conv_kernel(h_ref, w_ref, cw_ref, cb_ref, lg_ref, lb_ref, p_ref, co_ref,
                        a_scr, win_scr, y_scr, *, tm, ch, n_glu, rps, blocks_per_seq):
    i = pl.program_id(0)
    j = pl.program_id(1)

    @pl.when(j == 0)
    def _():
        @pl.when(i % blocks_per_seq == 0)
        def _():
            a_scr[0:CONV_HALO, :] = jnp.zeros((CONV_HALO, ch), F32)

        @pl.when(i % blocks_per_seq != 0)
        def _():
            a_scr[0:CONV_HALO, :] = a_scr[tm:tm + CONV_HALO, :]

    @pl.when(j < n_glu)
    def _():
        r = jnp.dot(h_ref[...], w_ref[...], preferred_element_type=F32)
        half = r.shape[1] // 2
        cols = pl.ds(pl.multiple_of(j * half, half), half)
        a_scr[CONV_HALO:CONV_HALO + tm, cols] = r[:, :half] * jax.nn.sigmoid(r[:, half:])

    @pl.when(j >= n_glu)
    def _():
        r0 = pl.multiple_of((j - n_glu) * rps, rps)
        win_scr[...] = a_scr[pl.ds(r0, rps + CONV_HALO), :]

        p_ref[...] = jnp.dot(h_ref[...], w_ref[...], preferred_element_type=F32).astype(p_ref.dtype)

        after = None
        for c in range(ch // LANES):
            lanes = slice(c * LANES, (c + 1) * LANES)
            for rt in range(rps // CONV_TILE_ROWS):
                base = rt * CONV_TILE_ROWS
                y = _conv_tile(win_scr, cw_ref, base, lanes, after) + cb_ref[:, lanes]
                y_scr[base:base + CONV_TILE_ROWS, lanes] = y
                after = _zero_after(y)

        for k in range(rps // LN_ROWS):
            y = y_scr[k * LN_ROWS:(k + 1) * LN_ROWS, :]
            mu = jnp.mean(y, axis=-1, keepdims=True)
            d = y - mu
            var = jnp.mean(d * d, axis=-1, keepdims=True)
            z = d * lax.rsqrt(var + LN_EPS) * lg_ref[...] + lb_ref[...]
            co_ref[k * LN_ROWS:(k + 1) * LN_ROWS, :] = _silu(z).astype(co_ref.dtype)


def _inproj_conv(h, w_perm, dw_w, dw_b, ln_g, ln_b, ch, seq, tm=1024, tn=1024):
    t, d = h.shape
    n = w_perm.shape[1]
    n_glu = 2 * ch // tn
    n_col = n // tn
    rps = tm // (n_col - n_glu)
    assert rps % CONV_TILE_ROWS == 0 and seq % tm == 0
    kern = functools.partial(_inproj_conv_kernel, tm=tm, ch=ch, n_glu=n_glu, rps=rps,
                             blocks_per_seq=seq // tm)
    vec = lambda: pl.BlockSpec((1, ch), lambda i, j: (0, 0))
    return pl.pallas_call(
        kern,
        grid=(t // tm, n_col),
        in_specs=[pl.BlockSpec((tm, d), lambda i, j: (i, 0), pipeline_mode=pl.Buffered(1)),
                  pl.BlockSpec((d, tn), lambda i, j: (0, j)),
                  pl.BlockSpec((CONV_WIDTH, ch), lambda i, j: (0, 0)),
                  vec(), vec(), vec()],
        out_specs=[pl.BlockSpec((tm, tn), lambda i, j: (i, jnp.maximum(j - n_glu, 0))),
                   pl.BlockSpec((rps, ch),
                                lambda i, j: (i * (n_col - n_glu) + jnp.maximum(j - n_glu, 0), 0))],
        out_shape=[jax.ShapeDtypeStruct((t, n - 2 * ch), BF16),
                   jax.ShapeDtypeStruct((t, ch), BF16)],
        scratch_shapes=[pltpu.VMEM((CONV_HALO + tm, ch), F32),
                        pltpu.VMEM((rps + CONV_HALO, ch), F32),
                        pltpu.VMEM((rps, ch), F32)],
        compiler_params=_params("arbitrary", "arbitrary"),
        name="in_proj_conv",
    )(h, w_perm, dw_w, dw_b, ln_g, ln_b)


def _permute_conv_columns(w, ch, tn):
    d = w.shape[0]
    half = tn // 2
    val = w[:, :ch].reshape(d, ch // half, half)
    gate = w[:, ch:2 * ch].reshape(d, ch // half, half)
    glu = jnp.concatenate([val, gate], axis=2).reshape(d, 2 * ch)
    return jnp.concatenate([glu, w[:, 2 * ch:]], axis=1)


def _rope_table_kernel(pos_ref, freq_ref, cos_ref, sin_ref):
    ang = pos_ref[...] * freq_ref[...]
    cos_ref[...] = jnp.cos(ang)
    sin_ref[...] = jnp.sin(ang)


def _rope_tables(pos_b, freqs, ts=512):
    t, half = pos_b.shape
    blk = lambda: pl.BlockSpec((ts, half), lambda i: (i, 0))
    return pl.pallas_call(
        _rope_table_kernel,
        grid=(t // ts,),
        in_specs=[blk(), pl.BlockSpec((1, half), lambda i: (0, 0))],
        out_specs=[blk(), blk()],
        out_shape=[jax.ShapeDtypeStruct((t, half), F32)] * 2,
        compiler_params=_params("arbitrary"),
        name="rope_tables",
    )(pos_b, freqs)


def _ret_kernel(q_ref, k_ref, v_ref, g_ref, cos_ref, sin_ref, dec_ref, xi_ref, zeta_ref, cd_ref,
                gn_ref, o_ref, state_ref, *, heads, hd, kscale):
    @pl.when(pl.program_id(1) == 0)
    def _():
        state_ref[...] = jnp.zeros(state_ref.shape, F32)

    half = hd // 2
    cos = cos_ref[...]
    sin = sin_ref[...]

    def rope(t_ref, lo):
        t1 = t_ref[:, lo:lo + half].astype(F32)
        t2 = t_ref[:, lo + half:lo + hd].astype(F32)
        return jnp.concatenate([t1 * cos - t2 * sin, t2 * cos + t1 * sin], axis=-1)

    for hh in range(heads):
        lo = hh * hd
        cols = slice(lo, lo + hd)
        qr = rope(q_ref, lo)
        kr = rope(k_ref, lo) * kscale
        v = v_ref[:, cols]

        scores = lax.dot_general(qr.astype(BF16), kr.astype(BF16), (((1,), (1,)), ((), ())),
                                 preferred_element_type=F32) * dec_ref[hh]
        inner = jnp.dot(scores.astype(BF16), v, preferred_element_type=F32)

        state = state_ref[hh]
        cross = jnp.dot((qr * xi_ref[hh]).astype(BF16), state.astype(BF16),
                        preferred_element_type=F32)
        kz_t = jnp.transpose(kr * zeta_ref[hh]).astype(BF16)
        state_ref[hh] = state * cd_ref[hh] + jnp.dot(kz_t, v, preferred_element_type=F32)

        out = inner + cross
        mu = jnp.mean(out, axis=-1, keepdims=True)
        d = out - mu
        var = jnp.mean(d * d, axis=-1, keepdims=True)
        y = d * lax.rsqrt(var + LN_EPS) * gn_ref[:, cols]
        o_ref[:, cols] = (_silu(g_ref[:, cols].astype(F32)) * y).astype(o_ref.dtype)


def _retention_group(p, cos_t, sin_t, dec, xi_b, zeta_b, cd_b, gn_g, batch, seq, col0, hd):
    t = p.shape[0]
    c = RET_CHUNK
    nc = seq // c
    h = RET_HEADS
    w = h * hd
    first = col0 // w
    row = lambda b, n: b * nc + n
    group_blk = lambda i: pl.BlockSpec((c, w), lambda b, n: (row(b, n), first + i))
    whole = lambda a: pl.BlockSpec(a.shape, lambda b, n: (0,) * a.ndim)
    rope_blk = lambda: pl.BlockSpec((c, hd // 2), lambda b, n: (row(b, n), 0))
    kern = functools.partial(_ret_kernel, heads=h, hd=hd, kscale=float(hd) ** -0.5)
    return pl.pallas_call(
        kern,
        grid=(batch, nc),
        in_specs=[group_blk(0), group_blk(1), group_blk(2), group_blk(3),
                  rope_blk(), rope_blk(),
                  whole(dec), whole(xi_b), whole(zeta_b), whole(cd_b), whole(gn_g)],
        out_specs=pl.BlockSpec((c, w), lambda b, n: (row(b, n), 0)),
        out_shape=jax.ShapeDtypeStruct((t, w), BF16),
        scratch_shapes=[pltpu.VMEM((h, hd, hd), F32)],
        compiler_params=_params("arbitrary", "arbitrary"),
        name="retention_group",
    )(p, p, p, p, cos_t, sin_t, dec, xi_b, zeta_b, cd_b, gn_g)


def _retention_tables(hd):
    c = RET_CHUNK
    heads = jnp.arange(RET_HEADS, dtype=F32)
    log_gamma = jnp.log(1.0 - 2.0 ** (-5.0 - heads))
    idx = jnp.arange(c)
    idx_f = idx.astype(F32)
    diff = (idx[:, None] - idx[None, :]).astype(F32)
    dec = jnp.where(diff[None] >= 0,
                    jnp.exp(jnp.maximum(diff, 0.0)[None] * log_gamma[:, None, None]), 0.0)
    xi = jnp.exp((idx_f + 1.0)[None] * log_gamma[:, None])
    zeta = jnp.exp((c - 1.0 - idx_f)[None] * log_gamma[:, None])
    cd = jnp.exp(c * log_gamma)
    bcast = lambda a: jnp.broadcast_to(a[:, :, None], (RET_HEADS, c, hd))
    return dec, bcast(xi), bcast(zeta), jnp.broadcast_to(cd[:, None, None], (RET_HEADS, 1, hd))


def _outproj_kernel(a1_ref, a2_ref, w1_ref, w2_ref, x_ref, gate_ref, o_ref):
    mix = (jnp.dot(a1_ref[...], w1_ref[...], preferred_element_type=F32)
           + jnp.dot(a2_ref[...], w2_ref[...], preferred_element_type=F32))
    o_ref[...] = x_ref[...] + gate_ref[...] * mix


def _outproj(a1, a2, w, x2, mod4, gate_idx, seq, tm=1024, tn=512):
    m, k1 = a1.shape
    k2 = a2.shape[1]
    n = w.shape[1]
    per_b = seq // tm
    return pl.pallas_call(
        _outproj_kernel,
        grid=(m // tm, n // tn),
        in_specs=[pl.BlockSpec((tm, k1), lambda i, j: (i, 0)),
                  pl.BlockSpec((tm, k2), lambda i, j: (i, 0)),
                  pl.BlockSpec((k1, tn), lambda i, j: (0, j)),
                  pl.BlockSpec((k2, tn), lambda i, j: (k1 // k2, j)),
                  pl.BlockSpec((tm, tn), lambda i, j: (i, j)),
                  pl.BlockSpec((None, None, 1, tn), lambda i, j: (i // per_b, gate_idx, 0, j))],
        out_specs=pl.BlockSpec((tm, tn), lambda i, j: (i, j)),
        out_shape=jax.ShapeDtypeStruct((m, n), F32),
        compiler_params=_params("arbitrary", "arbitrary"),
        name="out_proj",
    )(a1, a2, w, w, x2, mod4)


def _ffn_kernel(h_ref, w1_ref, w3_ref, w2_ref, o_ref, acc_ref):
    f = pl.program_id(1)

    @pl.when(f == 0)
    def _():
        acc_ref[...] = jnp.zeros(acc_ref.shape, F32)

    h = h_ref[...]
    a = jnp.dot(h, w1_ref[...], preferred_element_type=F32)
    b = jnp.dot(h, w3_ref[...], preferred_element_type=F32)
    u = (_silu(a) * b).astype(BF16)
    acc_ref[...] += jnp.dot(u, w2_ref[...], preferred_element_type=F32)

    @pl.when(f == pl.num_programs(1) - 1)
    def _():
        o_ref[...] = acc_ref[...].astype(o_ref.dtype)


def _ffn(h2, w1, w3, w2, tm=1024, tf=256):
    m, d = h2.shape
    f = w1.shape[1]
    once = pl.Buffered(1)
    return pl.pallas_call(
        _ffn_kernel,
        grid=(m // tm, f // tf),
        in_specs=[pl.BlockSpec((tm, d), lambda i, j: (i, 0), pipeline_mode=once),
                  pl.BlockSpec((d, tf), lambda i, j: (0, j)),
                  pl.BlockSpec((d, tf), lambda i, j: (0, j)),
                  pl.BlockSpec((tf, d), lambda i, j: (j, 0))],
        out_specs=pl.BlockSpec((tm, d), lambda i, j: (i, 0), pipeline_mode=once),
        out_shape=jax.ShapeDtypeStruct((m, d), BF16),
        scratch_shapes=[pltpu.VMEM((tm, d), F32)],
        compiler_params=_params("arbitrary", "arbitrary"),
        name="swiglu_ffn",
    )(h2, w1, w3, w2)


def _residual_kernel(x_ref, ff_ref, gate_ref, g_ref, o_ref, *, final_norm):
    y = x_ref[...] + gate_ref[...] * ff_ref[...].astype(F32)
    if final_norm:
        inv = lax.rsqrt(jnp.mean(y * y, axis=-1, keepdims=True) + EPS)
        y = y * inv * g_ref[...]
    o_ref[...] = y


def _residual(x1, ff, mod4, gate_idx, g, seq, final_norm, ts=256):
    t, d = x1.shape
    per_b = seq // ts
    return pl.pallas_call(
        functools.partial(_residual_kernel, final_norm=final_norm),
        grid=(t // ts,),
        in_specs=[pl.BlockSpec((ts, d), lambda i: (i, 0)),
                  pl.BlockSpec((ts, d), lambda i: (i, 0)),
                  pl.BlockSpec((None, None, 1, d), lambda i: (i // per_b, gate_idx, 0, 0)),
                  pl.BlockSpec((1, d), lambda i: (0, 0))],
        out_specs=pl.BlockSpec((ts, d), lambda i: (i, 0)),
        out_shape=jax.ShapeDtypeStruct((t, d), F32),
        compiler_params=_params("arbitrary"),
        name="final_norm",
    )(x1, ff, mod4, g)


def kernel(x, c, positions, norm1_g, norm2_g, normf_g, w_mod, b_mod, w_in, conv_dw_w, conv_dw_b,
           conv_ln_g, conv_ln_b, ret_gn_g, w_out, w_ffn1, w_ffn3, w_ffn2):
    batch, seq, d = x.shape
    depth = w_mod.shape[0]
    t = batch * seq
    conv_ch = conv_dw_w.shape[-1]
    ret_w = ret_gn_g.shape[-1]
    hd = ret_w // RET_HEADS

    half = hd // 2
    freqs = (ROPE_BASE ** (-jnp.arange(half, dtype=F32) / half)).reshape(1, half)
    pos_b = jnp.broadcast_to(positions.astype(F32).reshape(t, 1), (t, half))
    cos_t, sin_t = _rope_tables(pos_b, freqs)
    dec, xi_b, zeta_b, cd_b = _retention_tables(hd)

    c_pad = jnp.pad(c, ((0, SUBLANES - batch % SUBLANES), (0, 0))) if batch % SUBLANES else c
    x2 = x.reshape(t, d)
    row = lambda v: v.reshape(1, -1)

    for l in range(depth):
        mod = _modulation(c_pad, w_mod[l], row(b_mod[l]))
        mod4 = mod[:batch].reshape(batch, N_MOD, 1, d)

        h = _norm_mod(x2, row(norm1_g[l]), mod4, 1, 0, seq)
        w_perm = _permute_conv_columns(w_in[l], conv_ch, IN_PROJ_TN).astype(BF16)
        p, conv_out = _inproj_conv(h, w_perm, conv_dw_w[l], row(conv_dw_b[l]), row(conv_ln_g[l]),
                                   row(conv_ln_b[l]), conv_ch, seq, tm=IN_PROJ_TM, tn=IN_PROJ_TN)
        ret_out = _retention_group(p, cos_t, sin_t, dec, xi_b, zeta_b, cd_b, row(ret_gn_g[l]),
                                   batch, seq, 0, hd)

        x2 = _outproj(conv_out, ret_out, w_out[l].astype(BF16), x2, mod4, 2, seq)

        h2 = _norm_mod(x2, row(norm2_g[l]), mod4, 4, 3, seq)
        ff = _ffn(h2, w_ffn1[l].astype(BF16), w_ffn3[l].astype(BF16), w_ffn2[l].astype(BF16))
        x2 = _residual(x2, ff, mod4, 5, row(normf_g), seq, final_norm=(l + 1 == depth))

    return x2.reshape(batch, seq, d)
```

```python
import functools
import math

import jax
import jax.numpy as jnp
from jax import lax
from jax.experimental import pallas as pl
from jax.experimental.pallas import tpu as pltpu

F32 = jnp.float32
BF16 = jnp.bfloat16

N_MOD = 6
CONV_WIDTH = 31
RET_HEADS = 8
ROPE_BASE = 10000.0
EPS = 1e-6
LN_EPS = 1e-5

LANES = 128
SUBLANES = 8
VMEM_LIMIT_BYTES = 56 * 1024 * 1024

RET_CHUNK = 256
CONV_HALO = 32
CONV_TILE_ROWS = 32
LN_ROWS = 16
IN_PROJ_TM = 1024
IN_PROJ_TN = 1024


def _params(*semantics, flags=None):
    return pltpu.CompilerParams(dimension_semantics=semantics, vmem_limit_bytes=VMEM_LIMIT_BYTES,
                                flags=flags)


def _silu(x):
    return x * jax.nn.sigmoid(x)


def _mod_kernel(c_ref, w_ref, b_ref, o_ref):
    ca = _silu(c_ref[...]).astype(BF16)
    o_ref[...] = jnp.dot(ca, w_ref[...].astype(BF16), preferred_element_type=F32) + b_ref[...]


def _modulation(c_pad, w_mod, b_mod, tn=512):
    rows, d = c_pad.shape
    n = w_mod.shape[1]
    return pl.pallas_call(
        _mod_kernel,
        grid=(n // tn,),
        in_specs=[pl.BlockSpec((rows, d), lambda j: (0, 0)),
                  pl.BlockSpec((d, tn), lambda j: (0, j)),
                  pl.BlockSpec((1, tn), lambda j: (0, j))],
        out_specs=pl.BlockSpec((rows, tn), lambda j: (0, j)),
        out_shape=jax.ShapeDtypeStruct((rows, n), F32),
        compiler_params=_params("arbitrary"),
        name="adaln_mod",
    )(c_pad, w_mod, b_mod)


def _norm_mod_kernel(x_ref, g_ref, scale_ref, shift_ref, o_ref, gs_scr, sh_scr):
    strip = gs_scr.shape
    gs_scr[...] = jnp.broadcast_to(g_ref[...] * (1.0 + scale_ref[...]), strip)
    sh_scr[...] = jnp.broadcast_to(shift_ref[...], strip)
    for r in range(x_ref.shape[0] // LN_ROWS):
        rows = slice(r * LN_ROWS, (r + 1) * LN_ROWS)
        x = x_ref[rows, :]
        inv = lax.rsqrt(jnp.mean(x * x, axis=-1, keepdims=True) + EPS)
        o_ref[rows, :] = (x * inv * gs_scr[...] + sh_scr[...]).astype(o_ref.dtype)


def _norm_mod(x2, g, mod4, scale_idx, shift_idx, seq, ts=256):
    t, d = x2.shape
    per_b = seq // ts
    return pl.pallas_call(
        _norm_mod_kernel,
        grid=(t // ts,),
        in_specs=[pl.BlockSpec((ts, d), lambda i: (i, 0)),
                  pl.BlockSpec((1, d), lambda i: (0, 0)),
                  pl.BlockSpec((None, None, 1, d), lambda i: (i // per_b, scale_idx, 0, 0)),
                  pl.BlockSpec((None, None, 1, d), lambda i: (i // per_b, shift_idx, 0, 0))],
        out_specs=pl.BlockSpec((ts, d), lambda i: (i, 0)),
        out_shape=jax.ShapeDtypeStruct((t, d), BF16),
        scratch_shapes=[pltpu.VMEM((LN_ROWS, d), F32), pltpu.VMEM((LN_ROWS, d), F32)],
        compiler_params=_params("arbitrary"),
        name="rmsnorm_mod",
    )(x2, g, mod4, mod4)


def _zero_after(x):
    s = jnp.sum(x[0:SUBLANES, :], axis=1, keepdims=True)
    bits = lax.bitcast_convert_type(jnp.broadcast_to(s[0:1, :], (1, LANES)), jnp.uint32)
    sixteen = jnp.uint32(16)
    bits = lax.shift_right_logical(lax.shift_right_logical(bits, sixteen), sixteen)
    return lax.bitcast_convert_type(bits, F32)


def _conv_tile(win_ref, cw_ref, base, lanes, after=None):
    first = CONV_HALO - (CONV_WIDTH - 1)
    y = None
    for r in range(SUBLANES):
        off = first + r
        lo = base + (off // SUBLANES) * SUBLANES
        sub = off % SUBLANES
        span = CONV_TILE_ROWS + (SUBLANES if sub else 0)
        u = None
        for q in range((CONV_WIDTH - 1 - r) // SUBLANES + 1):
            tap = q * SUBLANES + r
            rows = slice(lo + q * SUBLANES, lo + q * SUBLANES + span)
            w_row = cw_ref[tap:tap + 1, lanes]
            if after is not None and tap == 0:
                w_row = w_row + after
            term = w_row * win_ref[rows, lanes]
            u = term if u is None else u + term
        u = u[sub:sub + CONV_TILE_ROWS, :]
        y = u if y is None else y + u
    return y


def _inproj_conv_kernel(h_ref, w_ref, cw_ref, cb_ref, lg_ref, lb_ref, p_ref, co_ref,
                        a_scr, win_scr, y_scr, *, tm, ch, n_glu, rps, blocks_per_seq):
    i = pl.program_id(0)
    j = pl.program_id(1)

    @pl.when(j == 0)
    def _():
        @pl.when(i % blocks_per_seq == 0)
        def _():
            a_scr[0:CONV_HALO, :] = jnp.zeros((CONV_HALO, ch), F32)

        @pl.when(i % blocks_per_seq != 0)
        def _():
            a_scr[0:CONV_HALO, :] = a_scr[tm:tm + CONV_HALO, :]

    body = slice(CONV_HALO, CONV_HALO + tm)
    tn = w_ref.shape[1]
    n_val = n_glu // 2

    @pl.when(j < n_val)
    def _():
        cols = pl.ds(pl.multiple_of(j * tn, tn), tn)
        a_scr[body, cols] = jnp.dot(h_ref[...], w_ref[...], preferred_element_type=F32)

    @pl.when(jnp.logical_and(j >= n_val, j < n_glu))
    def _():
        cols = pl.ds(pl.multiple_of((j - n_val) * tn, tn), tn)
        gate = jnp.dot(h_ref[...], w_ref[...], preferred_element_type=F32)
        a_scr[body, cols] = a_scr[body, cols] * jax.nn.sigmoid(gate)

    @pl.when(j >= n_glu)
    def _():
        r0 = pl.multiple_of((j - n_glu) * rps, rps)
        win_scr[...] = a_scr[pl.ds(r0, rps + CONV_HALO), :]

        p_ref[...] = jnp.dot(h_ref[...], w_ref[...], preferred_element_type=F32).astype(p_ref.dtype)

        after = None
        for c in range(ch // LANES):
            lanes = slice(c * LANES, (c + 1) * LANES)
            for rt in range(rps // CONV_TILE_ROWS):
                base = rt * CONV_TILE_ROWS
                y = _conv_tile(win_scr, cw_ref, base, lanes, after) + cb_ref[:, lanes]
                y_scr[base:base + CONV_TILE_ROWS, lanes] = y
                after = _zero_after(y)

        for k in range(rps // LN_ROWS):
            y = y_scr[k * LN_ROWS:(k + 1) * LN_ROWS, :]
            mu = jnp.mean(y, axis=-1, keepdims=True)
            d = y - mu
            var = jnp.mean(d * d, axis=-1, keepdims=True)
            z = d * lax.rsqrt(var + LN_EPS) * lg_ref[...] + lb_ref[...]
            co_ref[k * LN_ROWS:(k + 1) * LN_ROWS, :] = _silu(z).astype(co_ref.dtype)


def _inproj_conv(h, w, dw_w, dw_b, ln_g, ln_b, ch, seq, tm=1024, tn=1024):
    t, d = h.shape
    n = w.shape[1]
    n_glu = 2 * ch // tn
    n_col = n // tn
    rps = tm // (n_col - n_glu)
    assert rps % CONV_TILE_ROWS == 0 and seq % tm == 0 and ch % tn == 0
    kern = functools.partial(_inproj_conv_kernel, tm=tm, ch=ch, n_glu=n_glu, rps=rps,
                             blocks_per_seq=seq // tm)
    vec = lambda: pl.BlockSpec((1, ch), lambda i, j: (0, 0))
    return pl.pallas_call(
        kern,
        grid=(t // tm, n_col),
        in_specs=[pl.BlockSpec((tm, d), lambda i, j: (i, 0), pipeline_mode=pl.Buffered(1)),
                  pl.BlockSpec((d, tn), lambda i, j: (0, j)),
                  pl.BlockSpec((CONV_WIDTH, ch), lambda i, j: (0, 0)),
                  vec(), vec(), vec()],
        out_specs=[pl.BlockSpec((tm, tn), lambda i, j: (i, jnp.maximum(j - n_glu, 0))),
                   pl.BlockSpec((rps, ch),
                                lambda i, j: (i * (n_col - n_glu) + jnp.maximum(j - n_glu, 0), 0))],
        out_shape=[jax.ShapeDtypeStruct((t, n - 2 * ch), BF16),
                   jax.ShapeDtypeStruct((t, ch), BF16)],
        scratch_shapes=[pltpu.VMEM((CONV_HALO + tm, ch), F32),
                        pltpu.VMEM((rps + CONV_HALO, ch), F32),
                        pltpu.VMEM((rps, ch), F32)],
        compiler_params=_params("arbitrary", "arbitrary"),
        name="in_proj_conv",
    )(h, w, dw_w, dw_b, ln_g, ln_b)


def _rope_table_kernel(pos_ref, freq_ref, cos_ref, sin_ref):
    ang = pos_ref[...] * freq_ref[...]
    cos_ref[...] = jnp.cos(ang)
    sin_ref[...] = jnp.sin(ang)


def _rope_tables(pos_b, freqs, ts=512):
    t, half = pos_b.shape
    blk = lambda: pl.BlockSpec((ts, half), lambda i: (i, 0))
    return pl.pallas_call(
        _rope_table_kernel,
        grid=(t // ts,),
        in_specs=[blk(), pl.BlockSpec((1, half), lambda i: (0, 0))],
        out_specs=[blk(), blk()],
        out_shape=[jax.ShapeDtypeStruct((t, half), F32)] * 2,
        compiler_params=_params("arbitrary"),
        name="rope_tables",
    )(pos_b, freqs)


def _ret_kernel(q_ref, k_ref, v_ref, g_ref, cos_ref, sin_ref, dec_ref, xi_ref, zeta_ref, cd_ref,
                gn_ref, o_ref, state_ref, *, heads, hd, kscale):
    @pl.when(pl.program_id(1) == 0)
    def _():
        state_ref[...] = jnp.zeros(state_ref.shape, F32)

    half = hd // 2
    cos = cos_ref[...]
    sin = sin_ref[...]

    def rope(t_ref, lo):
        t1 = t_ref[:, lo:lo + half].astype(F32)
        t2 = t_ref[:, lo + half:lo + hd].astype(F32)
        return jnp.concatenate([t1 * cos - t2 * sin, t2 * cos + t1 * sin], axis=-1)

    for hh in range(heads):
        lo = hh * hd
        cols = slice(lo, lo + hd)
        qr = rope(q_ref, lo)
        kr = rope(k_ref, lo) * kscale
        v = v_ref[:, cols]

        scores = lax.dot_general(qr.astype(BF16), kr.astype(BF16), (((1,), (1,)), ((), ())),
                                 preferred_element_type=F32) * dec_ref[hh]
        inner = jnp.dot(scores.astype(BF16), v, preferred_element_type=F32)

        state = state_ref[hh]
        cross = jnp.dot((qr * xi_ref[hh]).astype(BF16), state.astype(BF16),
                        preferred_element_type=F32)
        kz_t = jnp.transpose(kr * zeta_ref[hh]).astype(BF16)
        state_ref[hh] = state * cd_ref[hh] + jnp.dot(kz_t, v, preferred_element_type=F32)

        out = inner + cross
        mu = jnp.mean(out, axis=-1, keepdims=True)
        d = out - mu
        var = jnp.mean(d * d, axis=-1, keepdims=True)
        y = d * lax.rsqrt(var + LN_EPS) * gn_ref[:, cols]
        o_ref[:, cols] = (_silu(g_ref[:, cols].astype(F32)) * y).astype(o_ref.dtype)


def _retention_group(p, cos_t, sin_t, dec, xi_b, zeta_b, cd_b, gn_g, batch, seq, col0, hd):
    t = p.shape[0]
    c = RET_CHUNK
    nc = seq // c
    h = RET_HEADS
    w = h * hd
    first = col0 // w
    row = lambda b, n: b * nc + n
    group_blk = lambda i: pl.BlockSpec((c, w), lambda b, n: (row(b, n), first + i))
    whole = lambda a: pl.BlockSpec(a.shape, lambda b, n: (0,) * a.ndim)
    rope_blk = lambda: pl.BlockSpec((c, hd // 2), lambda b, n: (row(b, n), 0))
    kern = functools.partial(_ret_kernel, heads=h, hd=hd, kscale=float(hd) ** -0.5)
    return pl.pallas_call(
        kern,
        grid=(batch, nc),
        in_specs=[group_blk(0), group_blk(1), group_blk(2), group_blk(3),
                  rope_blk(), rope_blk(),
                  whole(dec), whole(xi_b), whole(zeta_b), whole(cd_b), whole(gn_g)],
        out_specs=pl.BlockSpec((c, w), lambda b, n: (row(b, n), 0)),
        out_shape=jax.ShapeDtypeStruct((t, w), BF16),
        scratch_shapes=[pltpu.VMEM((h, hd, hd), F32)],
        compiler_params=_params("arbitrary", "arbitrary"),
        name="retention_group",
    )(p, p, p, p, cos_t, sin_t, dec, xi_b, zeta_b, cd_b, gn_g)


def _retention_tables(hd):
    c = RET_CHUNK
    heads = jnp.arange(RET_HEADS, dtype=F32)
    log_gamma = jnp.log(1.0 - 2.0 ** (-5.0 - heads))
    idx = jnp.arange(c)
    idx_f = idx.astype(F32)
    diff = (idx[:, None] - idx[None, :]).astype(F32)
    dec = jnp.where(diff[None] >= 0,
                    jnp.exp(jnp.maximum(diff, 0.0)[None] * log_gamma[:, None, None]), 0.0)
    xi = jnp.exp((idx_f + 1.0)[None] * log_gamma[:, None])
    zeta = jnp.exp((c - 1.0 - idx_f)[None] * log_gamma[:, None])
    cd = jnp.exp(c * log_gamma)
    bcast = lambda a: jnp.broadcast_to(a[:, :, None], (RET_HEADS, c, hd))
    return dec, bcast(xi), bcast(zeta), jnp.broadcast_to(cd[:, None, None], (RET_HEADS, 1, hd))


def _outproj_kernel(a1_ref, a2_ref, w1_ref, w2_ref, x_ref, gate_ref, o_ref):
    mix = (jnp.dot(a1_ref[...], w1_ref[...], preferred_element_type=F32)
           + jnp.dot(a2_ref[...], w2_ref[...], preferred_element_type=F32))
    o_ref[...] = x_ref[...] + gate_ref[...] * mix


def _outproj(a1, a2, w, x2, mod4, gate_idx, seq, tm=1024, tn=512):
    m, k1 = a1.shape
    k2 = a2.shape[1]
    n = w.shape[1]
    per_b = seq // tm
    return pl.pallas_call(
        _outproj_kernel,
        grid=(m // tm, n // tn),
        in_specs=[pl.BlockSpec((tm, k1), lambda i, j: (i, 0)),
                  pl.BlockSpec((tm, k2), lambda i, j: (i, 0)),
                  pl.BlockSpec((k1, tn), lambda i, j: (0, j)),
                  pl.BlockSpec((k2, tn), lambda i, j: (k1 // k2, j)),
                  pl.BlockSpec((tm, tn), lambda i, j: (i, j)),
                  pl.BlockSpec((None, None, 1, tn), lambda i, j: (i // per_b, gate_idx, 0, j))],
        out_specs=pl.BlockSpec((tm, tn), lambda i, j: (i, j)),
        out_shape=jax.ShapeDtypeStruct((m, n), F32),
        compiler_params=_params("arbitrary", "arbitrary"),
        name="out_proj",
    )(a1, a2, w, w, x2, mod4)


def _ffn_kernel(h_ref, w1_ref, w3_ref, w2_ref, o_ref, acc_ref):
    f = pl.program_id(1)

    @pl.when(f == 0)
    def _():
        acc_ref[...] = jnp.zeros(acc_ref.shape, F32)

    h = h_ref[...]
    a = jnp.dot(h, w1_ref[...], preferred_element_type=F32)
    b = jnp.dot(h, w3_ref[...], preferred_element_type=F32)
    u = (_silu(a) * b).astype(BF16)
    acc_ref[...] += jnp.dot(u, w2_ref[...], preferred_element_type=F32)

    @pl.when(f == pl.num_programs(1) - 1)
    def _():
        o_ref[...] = acc_ref[...].astype(o_ref.dtype)


def _ffn(h2, w1, w3, w2, tm=1024, tf=256):
    m, d = h2.shape
    f = w1.shape[1]
    once = pl.Buffered(1)
    return pl.pallas_call(
        _ffn_kernel,
        grid=(m // tm, f // tf),
        in_specs=[pl.BlockSpec((tm, d), lambda i, j: (i, 0), pipeline_mode=once),
                  pl.BlockSpec((d, tf), lambda i, j: (0, j)),
                  pl.BlockSpec((d, tf), lambda i, j: (0, j)),
                  pl.BlockSpec((tf, d), lambda i, j: (j, 0))],
        out_specs=pl.BlockSpec((tm, d), lambda i, j: (i, 0), pipeline_mode=once),
        out_shape=jax.ShapeDtypeStruct((m, d), BF16),
        scratch_shapes=[pltpu.VMEM((tm, d), F32)],
        compiler_params=_params("arbitrary", "arbitrary"),
        name="swiglu_ffn",
    )(h2, w1, w3, w2)


def _residual_kernel(x_ref, ff_ref, gate_ref, g_ref, o_ref, gate_scr, g_scr, *, final_norm):
    strip = gate_scr.shape
    gate_scr[...] = jnp.broadcast_to(gate_ref[...], strip)
    g_scr[...] = jnp.broadcast_to(g_ref[...], strip)
    for r in range(x_ref.shape[0] // LN_ROWS):
        rows = slice(r * LN_ROWS, (r + 1) * LN_ROWS)
        y = x_ref[rows, :] + gate_scr[...] * ff_ref[rows, :].astype(F32)
        if final_norm:
            inv = lax.rsqrt(jnp.mean(y * y, axis=-1, keepdims=True) + EPS)
            y = y * inv * g_scr[...]
        o_ref[rows, :] = y


def _residual(x1, ff, mod4, gate_idx, g, seq, final_norm, ts=256):
    t, d = x1.shape
    per_b = seq // ts
    return pl.pallas_call(
        functools.partial(_residual_kernel, final_norm=final_norm),
        grid=(t // ts,),
        in_specs=[pl.BlockSpec((ts, d), lambda i: (i, 0)),
                  pl.BlockSpec((ts, d), lambda i: (i, 0)),
                  pl.BlockSpec((None, None, 1, d), lambda i: (i // per_b, gate_idx, 0, 0)),
                  pl.BlockSpec((1, d), lambda i: (0, 0))],
        out_specs=pl.BlockSpec((ts, d), lambda i: (i, 0)),
        out_shape=jax.ShapeDtypeStruct((t, d), F32),
        scratch_shapes=[pltpu.VMEM((LN_ROWS, d), F32), pltpu.VMEM((LN_ROWS, d), F32)],
        compiler_params=_params("arbitrary"),
        name="final_norm",
    )(x1, ff, mod4, g)


def kernel(x, c, positions, norm1_g, norm2_g, normf_g, w_mod, b_mod, w_in, conv_dw_w, conv_dw_b,
           conv_ln_g, conv_ln_b, ret_gn_g, w_out, w_ffn1, w_ffn3, w_ffn2):
    batch, seq, d = x.shape
    depth = w_mod.shape[0]
    t = batch * seq
    conv_ch = conv_dw_w.shape[-1]
    ret_w = ret_gn_g.shape[-1]
    hd = ret_w // RET_HEADS

    half = hd // 2
    freqs = (ROPE_BASE ** (-jnp.arange(half, dtype=F32) / half)).reshape(1, half)
    pos_b = jnp.broadcast_to(positions.astype(F32).reshape(t, 1), (t, half))
    cos_t, sin_t = _rope_tables(pos_b, freqs)
    dec, xi_b, zeta_b, cd_b = _retention_tables(hd)

    c_pad = jnp.pad(c, ((0, SUBLANES - batch % SUBLANES), (0, 0))) if batch % SUBLANES else c
    x2 = x.reshape(t, d)
    row = lambda v: v.reshape(1, -1)

    for l in range(depth):
        mod = _modulation(c_pad, w_mod[l], row(b_mod[l]))
        mod4 = mod[:batch].reshape(batch, N_MOD, 1, d)

        h = _norm_mod(x2, row(norm1_g[l]), mod4, 1, 0, seq)
        p, conv_out = _inproj_conv(h, w_in[l].astype(BF16), conv_dw_w[l], row(conv_dw_b[l]),
                                   row(conv_ln_g[l]),
                                   row(conv_ln_b[l]), conv_ch, seq, tm=IN_PROJ_TM, tn=IN_PROJ_TN)
        ret_out = _retention_group(p, cos_t, sin_t, dec, xi_b, zeta_b, cd_b, row(ret_gn_g[l]),
                                   batch, seq, 0, hd)

        x2 = _outproj(conv_out, ret_out, w_out[l].astype(BF16), x2, mod4, 2, seq)

        h2 = _norm_mod(x2, row(norm2_g[l]), mod4, 4, 3, seq)
        ff = _ffn(h2, w_ffn1[l].astype(BF16), w_ffn3[l].astype(BF16), w_ffn2[l].astype(BF16))
        x2 = _residual(x2, ff, mod4, 5, row(normf_g), seq, final_norm=(l + 1 == depth))

    return x2.reshape(batch, seq, d)
```

```python
import functools
import math

import jax
import jax.numpy as jnp
from jax import lax
from jax.experimental import pallas as pl
from jax.experimental.pallas import tpu as pltpu

F32 = jnp.float32
BF16 = jnp.bfloat16

N_MOD = 6
CONV_WIDTH = 31
RET_HEADS = 8
ROPE_BASE = 10000.0
EPS = 1e-6
LN_EPS = 1e-5

LANES = 128
SUBLANES = 8
VMEM_LIMIT_BYTES = 56 * 1024 * 1024

RET_CHUNK = 256
CONV_HALO = 32
CONV_TILE_ROWS = 32
LN_ROWS = 16
IN_PROJ_TM = 1024
IN_PROJ_TN = 1024
FFN_TM = 1024
FFN_STRIP = 32
FFN_DELAY_LINKS = 16


def _params(*semantics, flags=None):
    return pltpu.CompilerParams(dimension_semantics=semantics, vmem_limit_bytes=VMEM_LIMIT_BYTES,
                                flags=flags)


def _silu(x):
    return x * jax.nn.sigmoid(x)


def _mod_kernel(c_ref, w_ref, b_ref, o_ref):
    ca = _silu(c_ref[...]).astype(BF16)
    o_ref[...] = jnp.dot(ca, w_ref[...].astype(BF16), preferred_element_type=F32) + b_ref[...]


def _modulation(c_pad, w_mod, b_mod, tn=512):
    rows, d = c_pad.shape
    n = w_mod.shape[1]
    return pl.pallas_call(
        _mod_kernel,
        grid=(n // tn,),
        in_specs=[pl.BlockSpec((rows, d), lambda j: (0, 0)),
                  pl.BlockSpec((d, tn), lambda j: (0, j)),
                  pl.BlockSpec((1, tn), lambda j: (0, j))],
        out_specs=pl.BlockSpec((rows, tn), lambda j: (0, j)),
        out_shape=jax.ShapeDtypeStruct((rows, n), F32),
        compiler_params=_params("arbitrary"),
        name="adaln_mod",
    )(c_pad, w_mod, b_mod)


def _norm_mod_kernel(x_ref, g_ref, scale_ref, shift_ref, o_ref, gs_scr, sh_scr):
    strip = gs_scr.shape
    gs_scr[...] = jnp.broadcast_to(g_ref[...] * (1.0 + scale_ref[...]), strip)
    sh_scr[...] = jnp.broadcast_to(shift_ref[...], strip)
    for r in range(x_ref.shape[0] // LN_ROWS):
        rows = slice(r * LN_ROWS, (r + 1) * LN_ROWS)
        x = x_ref[rows, :]
        inv = lax.rsqrt(jnp.mean(x * x, axis=-1, keepdims=True) + EPS)
        o_ref[rows, :] = (x * inv * gs_scr[...] + sh_scr[...]).astype(o_ref.dtype)


def _norm_mod(x2, g, mod4, scale_idx, shift_idx, seq, ts=256):
    t, d = x2.shape
    per_b = seq // ts
    return pl.pallas_call(
        _norm_mod_kernel,
        grid=(t // ts,),
        in_specs=[pl.BlockSpec((ts, d), lambda i: (i, 0)),
                  pl.BlockSpec((1, d), lambda i: (0, 0)),
                  pl.BlockSpec((None, None, 1, d), lambda i: (i // per_b, scale_idx, 0, 0)),
                  pl.BlockSpec((None, None, 1, d), lambda i: (i // per_b, shift_idx, 0, 0))],
        out_specs=pl.BlockSpec((ts, d), lambda i: (i, 0)),
        out_shape=jax.ShapeDtypeStruct((t, d), BF16),
        scratch_shapes=[pltpu.VMEM((LN_ROWS, d), F32), pltpu.VMEM((LN_ROWS, d), F32)],
        compiler_params=_params("arbitrary"),
        name="rmsnorm_mod",
    )(x2, g, mod4, mod4)


def _zero_after(x):
    s = jnp.sum(x[0:SUBLANES, :], axis=1, keepdims=True)
    bits = lax.bitcast_convert_type(jnp.broadcast_to(s[0:1, :], (1, LANES)), jnp.uint32)
    sixteen = jnp.uint32(16)
    bits = lax.shift_right_logical(lax.shift_right_logical(bits, sixteen), sixteen)
    return lax.bitcast_convert_type(bits, F32)


def _conv_tile(win_ref, cw_ref, base, lanes, after=None):
    first = CONV_HALO - (CONV_WIDTH - 1)
    y = None
    for r in range(SUBLANES):
        off = first + r
        lo = base + (off // SUBLANES) * SUBLANES
        sub = off % SUBLANES
        span = CONV_TILE_ROWS + (SUBLANES if sub else 0)
        u = None
        for q in range((CONV_WIDTH - 1 - r) // SUBLANES + 1):
            tap = q * SUBLANES + r
            rows = slice(lo + q * SUBLANES, lo + q * SUBLANES + span)
            w_row = cw_ref[tap:tap + 1, lanes]
            if after is not None and tap == 0:
                w_row = w_row + after
            term = w_row * win_ref[rows, lanes]
            u = term if u is None else u + term
        u = u[sub:sub + CONV_TILE_ROWS, :]
        y = u if y is None else y + u
    return y


def _inproj_conv_kernel(h_ref, w_ref, cw_ref, cb_ref, lg_ref, lb_ref, p_ref, co_ref,
                        a_scr, win_scr, y_scr, *, tm, ch, n_glu, rps, blocks_per_seq):
    i = pl.program_id(0)
    j = pl.program_id(1)

    @pl.when(j == 0)
    def _():
        @pl.when(i % blocks_per_seq == 0)
        def _():
            a_scr[0:CONV_HALO, :] = jnp.zeros((CONV_HALO, ch), F32)

        @pl.when(i % blocks_per_seq != 0)
        def _():
            a_scr[0:CONV_HALO, :] = a_scr[tm:tm + CONV_HALO, :]

    body = slice(CONV_HALO, CONV_HALO + tm)
    tn = w_ref.shape[1]
    n_val = n_glu // 2

    @pl.when(j < n_val)
    def _():
        cols = pl.ds(pl.multiple_of(j * tn, tn), tn)
        a_scr[body, cols] = jnp.dot(h_ref[...], w_ref[...], preferred_element_type=F32)

    @pl.when(jnp.logical_and(j >= n_val, j < n_glu))
    def _():
        cols = pl.ds(pl.multiple_of((j - n_val) * tn, tn), tn)
        gate = jnp.dot(h_ref[...], w_ref[...], preferred_element_type=F32)
        a_scr[body, cols] = a_scr[body, cols] * jax.nn.sigmoid(gate)

    @pl.when(j >= n_glu)
    def _():
        r0 = pl.multiple_of((j - n_glu) * rps, rps)
        win_scr[...] = a_scr[pl.ds(r0, rps + CONV_HALO), :]

        p_ref[...] = jnp.dot(h_ref[...], w_ref[...], preferred_element_type=F32).astype(p_ref.dtype)

        after = None
        for c in range(ch // LANES):
            lanes = slice(c * LANES, (c + 1) * LANES)
            for rt in range(rps // CONV_TILE_ROWS):
                base = rt * CONV_TILE_ROWS
                y = _conv_tile(win_scr, cw_ref, base, lanes, after) + cb_ref[:, lanes]
                y_scr[base:base + CONV_TILE_ROWS, lanes] = y
                after = _zero_after(y)

        for k in range(rps // LN_ROWS):
            y = y_scr[k * LN_ROWS:(k + 1) * LN_ROWS, :]
            mu = jnp.mean(y, axis=-1, keepdims=True)
            d = y - mu
            var = jnp.mean(d * d, axis=-1, keepdims=True)
            z = d * lax.rsqrt(var + LN_EPS) * lg_ref[...] + lb_ref[...]
            co_ref[k * LN_ROWS:(k + 1) * LN_ROWS, :] = _silu(z).astype(co_ref.dtype)


def _inproj_conv(h, w, dw_w, dw_b, ln_g, ln_b, ch, seq, tm=1024, tn=1024):
    t, d = h.shape
    n = w.shape[1]
    n_glu = 2 * ch // tn
    n_col = n // tn
    rps = tm // (n_col - n_glu)
    assert rps % CONV_TILE_ROWS == 0 and seq % tm == 0 and ch % tn == 0
    kern = functools.partial(_inproj_conv_kernel, tm=tm, ch=ch, n_glu=n_glu, rps=rps,
                             blocks_per_seq=seq // tm)
    vec = lambda: pl.BlockSpec((1, ch), lambda i, j: (0, 0))
    return pl.pallas_call(
        kern,
        grid=(t // tm, n_col),
        in_specs=[pl.BlockSpec((tm, d), lambda i, j: (i, 0), pipeline_mode=pl.Buffered(1)),
                  pl.BlockSpec((d, tn), lambda i, j: (0, j)),
                  pl.BlockSpec((CONV_WIDTH, ch), lambda i, j: (0, 0)),
                  vec(), vec(), vec()],
        out_specs=[pl.BlockSpec((tm, tn), lambda i, j: (i, jnp.maximum(j - n_glu, 0))),
                   pl.BlockSpec((rps, ch),
                                lambda i, j: (i * (n_col - n_glu) + jnp.maximum(j - n_glu, 0), 0))],
        out_shape=[jax.ShapeDtypeStruct((t, n - 2 * ch), BF16),
                   jax.ShapeDtypeStruct((t, ch), BF16)],
        scratch_shapes=[pltpu.VMEM((CONV_HALO + tm, ch), F32),
                        pltpu.VMEM((rps + CONV_HALO, ch), F32),
                        pltpu.VMEM((rps, ch), F32)],
        compiler_params=_params("arbitrary", "arbitrary"),
        name="in_proj_conv",
    )(h, w, dw_w, dw_b, ln_g, ln_b)


def _rope_table_kernel(pos_ref, freq_ref, cos_ref, sin_ref):
    ang = pos_ref[...] * freq_ref[...]
    cos_ref[...] = jnp.cos(ang)
    sin_ref[...] = jnp.sin(ang)


def _rope_tables(pos_b, freqs, ts=512):
    t, half = pos_b.shape
    blk = lambda: pl.BlockSpec((ts, half), lambda i: (i, 0))
    return pl.pallas_call(
        _rope_table_kernel,
        grid=(t // ts,),
        in_specs=[blk(), pl.BlockSpec((1, half), lambda i: (0, 0))],
        out_specs=[blk(), blk()],
        out_shape=[jax.ShapeDtypeStruct((t, half), F32)] * 2,
        compiler_params=_params("arbitrary"),
        name="rope_tables",
    )(pos_b, freqs)


def _ret_kernel(q_ref, k_ref, v_ref, g_ref, cos_ref, sin_ref, dec_ref, xi_ref, zeta_ref, cd_ref,
                gn_ref, o_ref, state_ref, *, heads, hd, kscale):
    @pl.when(pl.program_id(1) == 0)
    def _():
        state_ref[...] = jnp.zeros(state_ref.shape, F32)

    half = hd // 2
    cos = cos_ref[...]
    sin = sin_ref[...]

    def rope(t_ref, lo):
        t1 = t_ref[:, lo:lo + half].astype(F32)
        t2 = t_ref[:, lo + half:lo + hd].astype(F32)
        return jnp.concatenate([t1 * cos - t2 * sin, t2 * cos + t1 * sin], axis=-1)

    for hh in range(heads):
        lo = hh * hd
        cols = slice(lo, lo + hd)
        qr = rope(q_ref, lo)
        kr = rope(k_ref, lo) * kscale
        v = v_ref[:, cols]

        scores = lax.dot_general(qr.astype(BF16), kr.astype(BF16), (((1,), (1,)), ((), ())),
                                 preferred_element_type=F32) * dec_ref[hh]
        inner = jnp.dot(scores.astype(BF16), v, preferred_element_type=F32)

        state = state_ref[hh]
        cross = jnp.dot((qr * xi_ref[hh]).astype(BF16), state.astype(BF16),
                        preferred_element_type=F32)
        kz_t = jnp.transpose(kr * zeta_ref[hh]).astype(BF16)
        state_ref[hh] = state * cd_ref[hh] + jnp.dot(kz_t, v, preferred_element_type=F32)

        out = inner + cross
        mu = jnp.mean(out, axis=-1, keepdims=True)
        d = out - mu
        var = jnp.mean(d * d, axis=-1, keepdims=True)
        y = d * lax.rsqrt(var + LN_EPS) * gn_ref[:, cols]
        o_ref[:, cols] = (_silu(g_ref[:, cols].astype(F32)) * y).astype(o_ref.dtype)


def _retention_group(p, cos_t, sin_t, dec, xi_b, zeta_b, cd_b, gn_g, batch, seq, col0, hd):
    t = p.shape[0]
    c = RET_CHUNK
    nc = seq // c
    h = RET_HEADS
    w = h * hd
    first = col0 // w
    row = lambda b, n: b * nc + n
    group_blk = lambda i: pl.BlockSpec((c, w), lambda b, n: (row(b, n), first + i))
    whole = lambda a: pl.BlockSpec(a.shape, lambda b, n: (0,) * a.ndim)
    rope_blk = lambda: pl.BlockSpec((c, hd // 2), lambda b, n: (row(b, n), 0))
    kern = functools.partial(_ret_kernel, heads=h, hd=hd, kscale=float(hd) ** -0.5)
    return pl.pallas_call(
        kern,
        grid=(batch, nc),
        in_specs=[group_blk(0), group_blk(1), group_blk(2), group_blk(3),
                  rope_blk(), rope_blk(),
                  whole(dec), whole(xi_b), whole(zeta_b), whole(cd_b), whole(gn_g)],
        out_specs=pl.BlockSpec((c, w), lambda b, n: (row(b, n), 0)),
        out_shape=jax.ShapeDtypeStruct((t, w), BF16),
        scratch_shapes=[pltpu.VMEM((h, hd, hd), F32)],
        compiler_params=_params("arbitrary", "arbitrary"),
        name="retention_group",
    )(p, p, p, p, cos_t, sin_t, dec, xi_b, zeta_b, cd_b, gn_g)


def _retention_tables(hd):
    c = RET_CHUNK
    heads = jnp.arange(RET_HEADS, dtype=F32)
    log_gamma = jnp.log(1.0 - 2.0 ** (-5.0 - heads))
    idx = jnp.arange(c)
    idx_f = idx.astype(F32)
    diff = (idx[:, None] - idx[None, :]).astype(F32)
    dec = jnp.where(diff[None] >= 0,
                    jnp.exp(jnp.maximum(diff, 0.0)[None] * log_gamma[:, None, None]), 0.0)
    xi = jnp.exp((idx_f + 1.0)[None] * log_gamma[:, None])
    zeta = jnp.exp((c - 1.0 - idx_f)[None] * log_gamma[:, None])
    cd = jnp.exp(c * log_gamma)
    bcast = lambda a: jnp.broadcast_to(a[:, :, None], (RET_HEADS, c, hd))
    return dec, bcast(xi), bcast(zeta), jnp.broadcast_to(cd[:, None, None], (RET_HEADS, 1, hd))


def _outproj_kernel(a1_ref, a2_ref, w1_ref, w2_ref, x_ref, gate_ref, o_ref):
    mix = (jnp.dot(a1_ref[...], w1_ref[...], preferred_element_type=F32)
           + jnp.dot(a2_ref[...], w2_ref[...], preferred_element_type=F32))
    o_ref[...] = x_ref[...] + gate_ref[...] * mix


def _outproj(a1, a2, w, x2, mod4, gate_idx, seq, tm=1024, tn=512):
    m, k1 = a1.shape
    k2 = a2.shape[1]
    n = w.shape[1]
    per_b = seq // tm
    return pl.pallas_call(
        _outproj_kernel,
        grid=(m // tm, n // tn),
        in_specs=[pl.BlockSpec((tm, k1), lambda i, j: (i, 0)),
                  pl.BlockSpec((tm, k2), lambda i, j: (i, 0)),
                  pl.BlockSpec((k1, tn), lambda i, j: (0, j)),
                  pl.BlockSpec((k2, tn), lambda i, j: (k1 // k2, j)),
                  pl.BlockSpec((tm, tn), lambda i, j: (i, j)),
                  pl.BlockSpec((None, None, 1, tn), lambda i, j: (i // per_b, gate_idx, 0, j))],
        out_specs=pl.BlockSpec((tm, tn), lambda i, j: (i, j)),
        out_shape=jax.ShapeDtypeStruct((m, n), F32),
        compiler_params=_params("arbitrary", "arbitrary"),
        name="out_proj",
    )(a1, a2, w, w, x2, mod4)


def _delayed_zero(x, links):
    s = x[0:SUBLANES, 0:LANES].astype(F32)
    for _ in range(links):
        s = jnp.broadcast_to(jnp.sum(s, axis=1, keepdims=True) * (1.0 / LANES), (SUBLANES, LANES))
    bits = lax.bitcast_convert_type(s, jnp.uint32)
    sixteen = jnp.uint32(16)
    bits = lax.shift_right_logical(lax.shift_right_logical(bits, sixteen), sixteen)
    return lax.bitcast_convert_type(bits, F32)


def _ffn_kernel(x1_ref, g_ref, scale_ref, shift_ref, w1_ref, w3_ref, w2_ref, o_ref,
                h_scr, acc_ref, gs_scr, sh_scr, sink_scr, *, n_blocks, n_strips):
    r = pl.program_id(0)
    f = pl.program_id(1)
    norm_here = jnp.logical_and(r < n_blocks, f < n_strips)
    mm_here = r >= 1

    @pl.when(jnp.logical_and(r < n_blocks, f == 0))
    def _():
        gs_scr[...] = jnp.broadcast_to(g_ref[...] * (1.0 + scale_ref[...]), gs_scr.shape)
        sh_scr[...] = jnp.broadcast_to(shift_ref[...], sh_scr.shape)

    @pl.when(jnp.logical_and(mm_here, f == 0))
    def _():
        acc_ref[...] = jnp.zeros(acc_ref.shape, F32)

    def norm_strip():
        outs = []
        for k in range(FFN_STRIP // LN_ROWS):
            x = x1_ref[k * LN_ROWS:(k + 1) * LN_ROWS, :]
            inv = lax.rsqrt(jnp.mean(x * x, axis=-1, keepdims=True) + EPS)
            h = (x * inv * gs_scr[...] + sh_scr[...]).astype(h_scr.dtype)
            row0 = pl.multiple_of(f * FFN_STRIP + k * LN_ROWS, LN_ROWS)
            h_scr[r % 2, pl.ds(row0, LN_ROWS), :] = h
            outs.append(h)
        return outs

    def matmuls():
        h = h_scr[(r + 1) % 2]
        a = jnp.dot(h, w1_ref[...], preferred_element_type=F32)
        b = jnp.dot(h, w3_ref[...], preferred_element_type=F32)
        u = (_silu(a) * b).astype(BF16)
        acc_ref[...] += jnp.dot(u, w2_ref[...], preferred_element_type=F32)

    @pl.when(jnp.logical_and(norm_here, jnp.logical_not(mm_here)))
    def _():
        norm_strip()

    @pl.when(jnp.logical_and(norm_here, mm_here))
    def _():
        matmuls()
        outs = norm_strip()
        for k, h in enumerate(outs):
            sink_scr[k] = _delayed_zero(h, FFN_DELAY_LINKS * (k + 1))

    @pl.when(jnp.logical_and(jnp.logical_not(norm_here), mm_here))
    def _():
        matmuls()

    @pl.when(jnp.logical_and(mm_here, f == pl.num_programs(1) - 1))
    def _():
        o_ref[...] = acc_ref[...].astype(o_ref.dtype)


def _ffn(x1, g, mod4, scale_idx, shift_idx, w1, w3, w2, seq, tm=1024, tf=256):
    t, d = x1.shape
    nf = w1.shape[1] // tf
    n_blocks = t // tm
    n_strips = tm // FFN_STRIP
    per_b = seq // tm
    assert nf >= n_strips and seq % tm == 0
    last_strip = n_blocks * n_strips - 1
    blk = lambda r: jnp.minimum(r, n_blocks - 1)
    modvec = lambda idx: pl.BlockSpec((None, None, 1, d), lambda r, f: (blk(r) // per_b, idx, 0, 0))
    wcol = lambda r, f: (0, jnp.where(r == 0, 0, f))
    kern = functools.partial(_ffn_kernel, n_blocks=n_blocks, n_strips=n_strips)
    return pl.pallas_call(
        kern,
        grid=(n_blocks + 1, nf),
        in_specs=[pl.BlockSpec((FFN_STRIP, d),
                               lambda r, f: (jnp.where(r < n_blocks,
                                                       r * n_strips + jnp.minimum(f, n_strips - 1),
                                                       last_strip), 0)),
                  pl.BlockSpec((1, d), lambda r, f: (0, 0)),
                  modvec(scale_idx), modvec(shift_idx),
                  pl.BlockSpec((d, tf), wcol),
                  pl.BlockSpec((d, tf), wcol),
                  pl.BlockSpec((tf, d), lambda r, f: (jnp.where(r == 0, 0, f), 0))],
        out_specs=pl.BlockSpec((tm, d), lambda r, f: (jnp.maximum(r - 1, 0), 0),
                               pipeline_mode=pl.Buffered(1)),
        out_shape=jax.ShapeDtypeStruct((t, d), BF16),
        scratch_shapes=[pltpu.VMEM((2, tm, d), BF16),
                        pltpu.VMEM((tm, d), F32),
                        pltpu.VMEM((LN_ROWS, d), F32),
                        pltpu.VMEM((LN_ROWS, d), F32),
                        pltpu.VMEM((FFN_STRIP // LN_ROWS, SUBLANES, LANES), F32)],
        compiler_params=_params("arbitrary", "arbitrary"),
        name="swiglu_ffn",
    )(x1, g, mod4, mod4, w1, w3, w2)


def _residual_kernel(x_ref, ff_ref, gate_ref, g_ref, o_ref, gate_scr, g_scr, *, final_norm):
    strip = gate_scr.shape
    gate_scr[...] = jnp.broadcast_to(gate_ref[...], strip)
    g_scr[...] = jnp.broadcast_to(g_ref[...], strip)
    for r in range(x_ref.shape[0] // LN_ROWS):
        rows = slice(r * LN_ROWS, (r + 1) * LN_ROWS)
        y = x_ref[rows, :] + gate_scr[...] * ff_ref[rows, :].astype(F32)
        if final_norm:
            inv = lax.rsqrt(jnp.mean(y * y, axis=-1, keepdims=True) + EPS)
            y = y * inv * g_scr[...]
        o_ref[rows, :] = y


def _residual(x1, ff, mod4, gate_idx, g, seq, final_norm, ts=256):
    t, d = x1.shape
    per_b = seq // ts
    return pl.pallas_call(
        functools.partial(_residual_kernel, final_norm=final_norm),
        grid=(t // ts,),
        in_specs=[pl.BlockSpec((ts, d), lambda i: (i, 0)),
                  pl.BlockSpec((ts, d), lambda i: (i, 0)),
                  pl.BlockSpec((None, None, 1, d), lambda i: (i // per_b, gate_idx, 0, 0)),
                  pl.BlockSpec((1, d), lambda i: (0, 0))],
        out_specs=pl.BlockSpec((ts, d), lambda i: (i, 0)),
        out_shape=jax.ShapeDtypeStruct((t, d), F32),
        scratch_shapes=[pltpu.VMEM((LN_ROWS, d), F32), pltpu.VMEM((LN_ROWS, d), F32)],
        compiler_params=_params("arbitrary"),
        name="final_norm",
    )(x1, ff, mod4, g)


def kernel(x, c, positions, norm1_g, norm2_g, normf_g, w_mod, b_mod, w_in, conv_dw_w, conv_dw_b,
           conv_ln_g, conv_ln_b, ret_gn_g, w_out, w_ffn1, w_ffn3, w_ffn2):
    batch, seq, d = x.shape
    depth = w_mod.shape[0]
    t = batch * seq
    conv_ch = conv_dw_w.shape[-1]
    ret_w = ret_gn_g.shape[-1]
    hd = ret_w // RET_HEADS

    half = hd // 2
    freqs = (ROPE_BASE ** (-jnp.arange(half, dtype=F32) / half)).reshape(1, half)
    pos_b = jnp.broadcast_to(positions.astype(F32).reshape(t, 1), (t, half))
    cos_t, sin_t = _rope_tables(pos_b, freqs)
    dec, xi_b, zeta_b, cd_b = _retention_tables(hd)

    c_pad = jnp.pad(c, ((0, SUBLANES - batch % SUBLANES), (0, 0))) if batch % SUBLANES else c
    x2 = x.reshape(t, d)
    row = lambda v: v.reshape(1, -1)

    for l in range(depth):
        mod = _modulation(c_pad, w_mod[l], row(b_mod[l]))
        mod4 = mod[:batch].reshape(batch, N_MOD, 1, d)

        h = _norm_mod(x2, row(norm1_g[l]), mod4, 1, 0, seq)
        p, conv_out = _inproj_conv(h, w_in[l].astype(BF16), conv_dw_w[l], row(conv_dw_b[l]),
                                   row(conv_ln_g[l]),
                                   row(conv_ln_b[l]), conv_ch, seq, tm=IN_PROJ_TM, tn=IN_PROJ_TN)
        ret_out = _retention_group(p, cos_t, sin_t, dec, xi_b, zeta_b, cd_b, row(ret_gn_g[l]),
                                   batch, seq, 0, hd)

        x2 = _outproj(conv_out, ret_out, w_out[l].astype(BF16), x2, mod4, 2, seq)

        ff = _ffn(x2, row(norm2_g[l]), mod4, 4, 3, w_ffn1[l].astype(BF16), w_ffn3[l].astype(BF16),
                  w_ffn2[l].astype(BF16), seq, tm=FFN_TM)
        x2 = _residual(x2, ff, mod4, 5, row(normf_g), seq, final_norm=(l + 1 == depth))

    return x2.reshape(batch, seq, d)
```

```python
import functools
import math

import jax
import jax.numpy as jnp
from jax import lax
from jax.experimental import pallas as pl
from jax.experimental.pallas import tpu as pltpu

F32 = jnp.float32
BF16 = jnp.bfloat16

N_MOD = 6
CONV_WIDTH = 31
RET_HEADS = 8
ROPE_BASE = 10000.0
EPS = 1e-6
LN_EPS = 1e-5

LANES = 128
SUBLANES = 8
VMEM_LIMIT_BYTES = 56 * 1024 * 1024
FFN_VMEM_LIMIT_BYTES = 60 * 1024 * 1024

RET_CHUNK = 256
CONV_HALO = 32
CONV_TILE_ROWS = 32
LN_ROWS = 16
IN_PROJ_TM = 1024
IN_PROJ_TN = 1024
FFN_TM = 1024
FFN_STRIP = 32
FFN_DELAY_LINKS = 16


def _params(*semantics, vmem=VMEM_LIMIT_BYTES):
    return pltpu.CompilerParams(dimension_semantics=semantics, vmem_limit_bytes=vmem)


def _silu(x):
    return x * jax.nn.sigmoid(x)


def _mod_kernel(c_ref, w_ref, b_ref, o_ref):
    ca = _silu(c_ref[...]).astype(BF16)
    o_ref[...] = jnp.dot(ca, w_ref[...].astype(BF16), preferred_element_type=F32) + b_ref[...]


def _modulation(c_pad, w_mod, b_mod, tn=512):
    rows, d = c_pad.shape
    n = w_mod.shape[1]
    return pl.pallas_call(
        _mod_kernel,
        grid=(n // tn,),
        in_specs=[pl.BlockSpec((rows, d), lambda j: (0, 0)),
                  pl.BlockSpec((d, tn), lambda j: (0, j)),
                  pl.BlockSpec((1, tn), lambda j: (0, j))],
        out_specs=pl.BlockSpec((rows, tn), lambda j: (0, j)),
        out_shape=jax.ShapeDtypeStruct((rows, n), F32),
        compiler_params=_params("arbitrary"),
        name="adaln_mod",
    )(c_pad, w_mod, b_mod)


def _norm_mod_kernel(x_ref, g_ref, scale_ref, shift_ref, o_ref, gs_scr, sh_scr):
    strip = gs_scr.shape
    gs_scr[...] = jnp.broadcast_to(g_ref[...] * (1.0 + scale_ref[...]), strip)
    sh_scr[...] = jnp.broadcast_to(shift_ref[...], strip)
    for r in range(x_ref.shape[0] // LN_ROWS):
        rows = slice(r * LN_ROWS, (r + 1) * LN_ROWS)
        x = x_ref[rows, :]
        inv = lax.rsqrt(jnp.mean(x * x, axis=-1, keepdims=True) + EPS)
        o_ref[rows, :] = (x * inv * gs_scr[...] + sh_scr[...]).astype(o_ref.dtype)


def _norm_mod(x2, g, mod4, scale_idx, shift_idx, seq, ts=256):
    t, d = x2.shape
    per_b = seq // ts
    return pl.pallas_call(
        _norm_mod_kernel,
        grid=(t // ts,),
        in_specs=[pl.BlockSpec((ts, d), lambda i: (i, 0)),
                  pl.BlockSpec((1, d), lambda i: (0, 0)),
                  pl.BlockSpec((None, None, 1, d), lambda i: (i // per_b, scale_idx, 0, 0)),
                  pl.BlockSpec((None, None, 1, d), lambda i: (i // per_b, shift_idx, 0, 0))],
        out_specs=pl.BlockSpec((ts, d), lambda i: (i, 0)),
        out_shape=jax.ShapeDtypeStruct((t, d), BF16),
        scratch_shapes=[pltpu.VMEM((LN_ROWS, d), F32), pltpu.VMEM((LN_ROWS, d), F32)],
        compiler_params=_params("arbitrary"),
        name="rmsnorm_mod",
    )(x2, g, mod4, mod4)


def _zero_after(x):
    s = jnp.sum(x[0:SUBLANES, :], axis=1, keepdims=True)
    bits = lax.bitcast_convert_type(jnp.broadcast_to(s[0:1, :], (1, LANES)), jnp.uint32)
    sixteen = jnp.uint32(16)
    bits = lax.shift_right_logical(lax.shift_right_logical(bits, sixteen), sixteen)
    return lax.bitcast_convert_type(bits, F32)


def _conv_tile(win_ref, cw_ref, base, lanes, after=None):
    first = CONV_HALO - (CONV_WIDTH - 1)
    y = None
    for r in range(SUBLANES):
        off = first + r
        lo = base + (off // SUBLANES) * SUBLANES
        sub = off % SUBLANES
        span = CONV_TILE_ROWS + (SUBLANES if sub else 0)
        u = None
        for q in range((CONV_WIDTH - 1 - r) // SUBLANES + 1):
            tap = q * SUBLANES + r
            rows = slice(lo + q * SUBLANES, lo + q * SUBLANES + span)
            w_row = cw_ref[tap:tap + 1, lanes]
            if after is not None and tap == 0:
                w_row = w_row + after
            term = w_row * win_ref[rows, lanes]
            u = term if u is None else u + term
        u = u[sub:sub + CONV_TILE_ROWS, :]
        y = u if y is None else y + u
    return y


def _inproj_conv_kernel(h_ref, w_ref, cw_ref, cb_ref, lg_ref, lb_ref, p_ref, co_ref,
                        a_scr, win_scr, y_scr, *, tm, ch, n_glu, rps, blocks_per_seq):
    i = pl.program_id(0)
    j = pl.program_id(1)

    @pl.when(j == 0)
    def _():
        @pl.when(i % blocks_per_seq == 0)
        def _():
            a_scr[0:CONV_HALO, :] = jnp.zeros((CONV_HALO, ch), F32)

        @pl.when(i % blocks_per_seq != 0)
        def _():
            a_scr[0:CONV_HALO, :] = a_scr[tm:tm + CONV_HALO, :]

    body = slice(CONV_HALO, CONV_HALO + tm)
    tn = w_ref.shape[1]
    n_val = n_glu // 2

    @pl.when(j < n_val)
    def _():
        cols = pl.ds(pl.multiple_of(j * tn, tn), tn)
        a_scr[body, cols] = jnp.dot(h_ref[...], w_ref[...], preferred_element_type=F32)

    @pl.when(jnp.logical_and(j >= n_val, j < n_glu))
    def _():
        cols = pl.ds(pl.multiple_of((j - n_val) * tn, tn), tn)
        gate = jnp.dot(h_ref[...], w_ref[...], preferred_element_type=F32)
        a_scr[body, cols] = a_scr[body, cols] * jax.nn.sigmoid(gate)

    @pl.when(j >= n_glu)
    def _():
        r0 = pl.multiple_of((j - n_glu) * rps, rps)
        win_scr[...] = a_scr[pl.ds(r0, rps + CONV_HALO), :]

        p_ref[...] = jnp.dot(h_ref[...], w_ref[...], preferred_element_type=F32).astype(p_ref.dtype)

        after = None
        for c in range(ch // LANES):
            lanes = slice(c * LANES, (c + 1) * LANES)
            for rt in range(rps // CONV_TILE_ROWS):
                base = rt * CONV_TILE_ROWS
                y = _conv_tile(win_scr, cw_ref, base, lanes, after) + cb_ref[:, lanes]
                y_scr[base:base + CONV_TILE_ROWS, lanes] = y
                after = _zero_after(y)

        for k in range(rps // LN_ROWS):
            y = y_scr[k * LN_ROWS:(k + 1) * LN_ROWS, :]
            mu = jnp.mean(y, axis=-1, keepdims=True)
            d = y - mu
            var = jnp.mean(d * d, axis=-1, keepdims=True)
            z = d * lax.rsqrt(var + LN_EPS) * lg_ref[...] + lb_ref[...]
            co_ref[k * LN_ROWS:(k + 1) * LN_ROWS, :] = _silu(z).astype(co_ref.dtype)


def _inproj_conv(h, w, dw_w, dw_b, ln_g, ln_b, ch, seq, tm=1024, tn=1024):
    t, d = h.shape
    n = w.shape[1]
    n_glu = 2 * ch // tn
    n_col = n // tn
    rps = tm // (n_col - n_glu)
    assert rps % CONV_TILE_ROWS == 0 and seq % tm == 0 and ch % tn == 0
    kern = functools.partial(_inproj_conv_kernel, tm=tm, ch=ch, n_glu=n_glu, rps=rps,
                             blocks_per_seq=seq // tm)
    vec = lambda: pl.BlockSpec((1, ch), lambda i, j: (0, 0))
    return pl.pallas_call(
        kern,
        grid=(t // tm, n_col),
        in_specs=[pl.BlockSpec((tm, d), lambda i, j: (i, 0), pipeline_mode=pl.Buffered(1)),
                  pl.BlockSpec((d, tn), lambda i, j: (0, j)),
                  pl.BlockSpec((CONV_WIDTH, ch), lambda i, j: (0, 0)),
                  vec(), vec(), vec()],
        out_specs=[pl.BlockSpec((tm, tn), lambda i, j: (i, jnp.maximum(j - n_glu, 0))),
                   pl.BlockSpec((rps, ch),
                                lambda i, j: (i * (n_col - n_glu) + jnp.maximum(j - n_glu, 0), 0))],
        out_shape=[jax.ShapeDtypeStruct((t, n - 2 * ch), BF16),
                   jax.ShapeDtypeStruct((t, ch), BF16)],
        scratch_shapes=[pltpu.VMEM((CONV_HALO + tm, ch), F32),
                        pltpu.VMEM((rps + CONV_HALO, ch), F32),
                        pltpu.VMEM((rps, ch), F32)],
        compiler_params=_params("arbitrary", "arbitrary"),
        name="in_proj_conv",
    )(h, w, dw_w, dw_b, ln_g, ln_b)


def _rope_table_kernel(pos_ref, freq_ref, cos_ref, sin_ref):
    ang = pos_ref[...] * freq_ref[...]
    cos_ref[...] = jnp.cos(ang)
    sin_ref[...] = jnp.sin(ang)


def _rope_tables(pos_b, freqs, ts=512):
    t, half = pos_b.shape
    blk = lambda: pl.BlockSpec((ts, half), lambda i: (i, 0))
    return pl.pallas_call(
        _rope_table_kernel,
        grid=(t // ts,),
        in_specs=[blk(), pl.BlockSpec((1, half), lambda i: (0, 0))],
        out_specs=[blk(), blk()],
        out_shape=[jax.ShapeDtypeStruct((t, half), F32)] * 2,
        compiler_params=_params("arbitrary"),
        name="rope_tables",
    )(pos_b, freqs)


def _ret_kernel(q_ref, k_ref, v_ref, g_ref, cos_ref, sin_ref, dec_ref, xi_ref, zeta_ref, cd_ref,
                gn_ref, o_ref, state_ref, *, heads, hd, kscale):
    @pl.when(pl.program_id(1) == 0)
    def _():
        state_ref[...] = jnp.zeros(state_ref.shape, F32)

    half = hd // 2
    cos = cos_ref[...]
    sin = sin_ref[...]

    def rope(t_ref, lo):
        t1 = t_ref[:, lo:lo + half].astype(F32)
        t2 = t_ref[:, lo + half:lo + hd].astype(F32)
        return jnp.concatenate([t1 * cos - t2 * sin, t2 * cos + t1 * sin], axis=-1)

    for hh in range(heads):
        lo = hh * hd
        cols = slice(lo, lo + hd)
        qr = rope(q_ref, lo)
        kr = rope(k_ref, lo) * kscale
        v = v_ref[:, cols]

        scores = lax.dot_general(qr.astype(BF16), kr.astype(BF16), (((1,), (1,)), ((), ())),
                                 preferred_element_type=F32) * dec_ref[hh]
        inner = jnp.dot(scores.astype(BF16), v, preferred_element_type=F32)

        state = state_ref[hh]
        cross = jnp.dot((qr * xi_ref[hh]).astype(BF16), state.astype(BF16),
                        preferred_element_type=F32)
        kz_t = jnp.transpose(kr * zeta_ref[hh]).astype(BF16)
        state_ref[hh] = state * cd_ref[hh] + jnp.dot(kz_t, v, preferred_element_type=F32)

        out = inner + cross
        mu = jnp.mean(out, axis=-1, keepdims=True)
        d = out - mu
        var = jnp.mean(d * d, axis=-1, keepdims=True)
        y = d * lax.rsqrt(var + LN_EPS) * gn_ref[:, cols]
        o_ref[:, cols] = (_silu(g_ref[:, cols].astype(F32)) * y).astype(o_ref.dtype)


def _retention_group(p, cos_t, sin_t, dec, xi_b, zeta_b, cd_b, gn_g, batch, seq, col0, hd):
    t = p.shape[0]
    c = RET_CHUNK
    nc = seq // c
    h = RET_HEADS
    w = h * hd
    first = col0 // w
    row = lambda b, n: b * nc + n
    group_blk = lambda i: pl.BlockSpec((c, w), lambda b, n: (row(b, n), first + i))
    whole = lambda a: pl.BlockSpec(a.shape, lambda b, n: (0,) * a.ndim)
    rope_blk = lambda: pl.BlockSpec((c, hd // 2), lambda b, n: (row(b, n), 0))
    kern = functools.partial(_ret_kernel, heads=h, hd=hd, kscale=float(hd) ** -0.5)
    return pl.pallas_call(
        kern,
        grid=(batch, nc),
        in_specs=[group_blk(0), group_blk(1), group_blk(2), group_blk(3),
                  rope_blk(), rope_blk(),
                  whole(dec), whole(xi_b), whole(zeta_b), whole(cd_b), whole(gn_g)],
        out_specs=pl.BlockSpec((c, w), lambda b, n: (row(b, n), 0)),
        out_shape=jax.ShapeDtypeStruct((t, w), BF16),
        scratch_shapes=[pltpu.VMEM((h, hd, hd), F32)],
        compiler_params=_params("arbitrary", "arbitrary"),
        name="retention_group",
    )(p, p, p, p, cos_t, sin_t, dec, xi_b, zeta_b, cd_b, gn_g)


def _retention_tables(hd):
    c = RET_CHUNK
    heads = jnp.arange(RET_HEADS, dtype=F32)
    log_gamma = jnp.log(1.0 - 2.0 ** (-5.0 - heads))
    idx = jnp.arange(c)
    idx_f = idx.astype(F32)
    diff = (idx[:, None] - idx[None, :]).astype(F32)
    dec = jnp.where(diff[None] >= 0,
                    jnp.exp(jnp.maximum(diff, 0.0)[None] * log_gamma[:, None, None]), 0.0)
    xi = jnp.exp((idx_f + 1.0)[None] * log_gamma[:, None])
    zeta = jnp.exp((c - 1.0 - idx_f)[None] * log_gamma[:, None])
    cd = jnp.exp(c * log_gamma)
    bcast = lambda a: jnp.broadcast_to(a[:, :, None], (RET_HEADS, c, hd))
    return dec, bcast(xi), bcast(zeta), jnp.broadcast_to(cd[:, None, None], (RET_HEADS, 1, hd))


def _outproj_kernel(a1_ref, a2_ref, w1_ref, w2_ref, x_ref, gate_ref, o_ref):
    mix = (jnp.dot(a1_ref[...], w1_ref[...], preferred_element_type=F32)
           + jnp.dot(a2_ref[...], w2_ref[...], preferred_element_type=F32))
    o_ref[...] = x_ref[...] + gate_ref[...] * mix


def _outproj(a1, a2, w, x2, mod4, gate_idx, seq, tm=1024, tn=512):
    m, k1 = a1.shape
    k2 = a2.shape[1]
    n = w.shape[1]
    per_b = seq // tm
    return pl.pallas_call(
        _outproj_kernel,
        grid=(m // tm, n // tn),
        in_specs=[pl.BlockSpec((tm, k1), lambda i, j: (i, 0)),
                  pl.BlockSpec((tm, k2), lambda i, j: (i, 0)),
                  pl.BlockSpec((k1, tn), lambda i, j: (0, j)),
                  pl.BlockSpec((k2, tn), lambda i, j: (k1 // k2, j)),
                  pl.BlockSpec((tm, tn), lambda i, j: (i, j)),
                  pl.BlockSpec((None, None, 1, tn), lambda i, j: (i // per_b, gate_idx, 0, j))],
        out_specs=pl.BlockSpec((tm, tn), lambda i, j: (i, j)),
        out_shape=jax.ShapeDtypeStruct((m, n), F32),
        compiler_params=_params("arbitrary", "arbitrary"),
        name="out_proj",
    )(a1, a2, w, w, x2, mod4)


def _delayed_zero(x, links):
    s0 = x[0:SUBLANES, 0:LANES].astype(F32)
    s = s0
    for _ in range(links):
        s = (pltpu.roll(s, 1, axis=1) + s0) * 0.5
    bits = lax.bitcast_convert_type(s, jnp.uint32)
    sixteen = jnp.uint32(16)
    bits = lax.shift_right_logical(lax.shift_right_logical(bits, sixteen), sixteen)
    return lax.bitcast_convert_type(bits, F32)


def _ffn_kernel(xa_ref, xb_ref, g2_ref, scale_ref, shift_ref, gate_ref, gf_ref, w1_ref, w3_ref, w2_ref,
                o_ref, h_scr, acc_ref, ff_scr, gs_scr, sh_scr, gate_scr, gf_scr,
                *, n_blocks, n_strips, final_norm):
    r = pl.program_id(0)
    f = pl.program_id(1)
    strips_here = f < n_strips
    mm_here = jnp.logical_and(r >= 1, r <= n_blocks)
    halves = FFN_STRIP // LN_ROWS

    @pl.when(f == 0)
    def _():
        gs_scr[...] = jnp.broadcast_to(g2_ref[...] * (1.0 + scale_ref[...]), gs_scr.shape)
        sh_scr[...] = jnp.broadcast_to(shift_ref[...], sh_scr.shape)
        gate_scr[...] = jnp.broadcast_to(gate_ref[...], gate_scr.shape)
        gf_scr[...] = jnp.broadcast_to(gf_ref[...], gf_scr.shape)

    @pl.when(jnp.logical_and(r == 0, f == 0))
    def _():
        ff_scr[...] = jnp.zeros(ff_scr.shape, ff_scr.dtype)

    @pl.when(jnp.logical_and(mm_here, f == 0))
    def _():
        acc_ref[...] = jnp.zeros(acc_ref.shape, F32)

    def strips():
        outs = []
        for k in range(halves):
            rows = slice(k * LN_ROWS, (k + 1) * LN_ROWS)
            row0 = pl.multiple_of(f * FFN_STRIP + k * LN_ROWS, LN_ROWS)
            x = xa_ref[rows, :]
            inv = lax.rsqrt(jnp.mean(x * x, axis=-1, keepdims=True) + EPS)
            h = (x * inv * gs_scr[...] + sh_scr[...]).astype(h_scr.dtype)
            h_scr[r % 2, pl.ds(row0, LN_ROWS), :] = h
            y = xb_ref[rows, :] + gate_scr[...] * ff_scr[pl.ds(row0, LN_ROWS), :].astype(F32)
            if final_norm:
                inv = lax.rsqrt(jnp.mean(y * y, axis=-1, keepdims=True) + EPS)
                y = y * inv * gf_scr[...]
            o_ref[rows, :] = y
            outs += [h, y]
        return outs

    def matmuls():
        h = h_scr[(r + 1) % 2]
        a = jnp.dot(h, w1_ref[...], preferred_element_type=F32)
        b = jnp.dot(h, w3_ref[...], preferred_element_type=F32)
        u = (_silu(a) * b).astype(BF16)
        acc_ref[...] += jnp.dot(u, w2_ref[...], preferred_element_type=F32)

    @pl.when(jnp.logical_and(strips_here, jnp.logical_not(mm_here)))
    def _():
        strips()

    @pl.when(jnp.logical_and(strips_here, mm_here))
    def _():
        matmuls()
        zero = None
        for k, v in enumerate(strips()):
            z = _delayed_zero(v, FFN_DELAY_LINKS * (k + 1))
            zero = z if zero is None else zero + z
        acc_ref[0:SUBLANES, 0:LANES] += zero

    @pl.when(jnp.logical_and(jnp.logical_not(strips_here), mm_here))
    def _():
        matmuls()

    @pl.when(jnp.logical_and(mm_here, f == pl.num_programs(1) - 1))
    def _():
        ff_scr[...] = acc_ref[...].astype(ff_scr.dtype)


def _ffn(x1, g2, mod4, scale_idx, shift_idx, gate_idx, gf, w1, w3, w2, seq, final_norm,
         tm=1024, tf=256):
    t, d = x1.shape
    nf = w1.shape[1] // tf
    n_blocks = t // tm
    n_strips = tm // FFN_STRIP
    per_b = seq // tm
    assert nf >= n_strips and seq % tm == 0
    blk = lambda r: jnp.clip(r, 0, n_blocks - 1)
    step = lambda f: jnp.minimum(f, n_strips - 1)
    strip = lambda b, f: jnp.where(b < 0, 0, jnp.where(b < n_blocks, blk(b) * n_strips + step(f),
                                                       n_blocks * n_strips - 1))
    modvec = lambda idx, lag: pl.BlockSpec((None, None, 1, d),
                                           lambda r, f: (blk(r - lag) // per_b, idx, 0, 0))
    wtile = lambda r, f: jnp.where(r == 0, 0, jnp.where(r > n_blocks, nf - 1, f))
    kern = functools.partial(_ffn_kernel, n_blocks=n_blocks, n_strips=n_strips,
                             final_norm=final_norm)
    vec_scr = lambda: pltpu.VMEM((LN_ROWS, d), F32)
    return pl.pallas_call(
        kern,
        grid=(n_blocks + 2, nf),
        in_specs=[pl.BlockSpec((FFN_STRIP, d), lambda r, f: (strip(r, f), 0)),
                  pl.BlockSpec((FFN_STRIP, d), lambda r, f: (strip(r - 2, f), 0)),
                  pl.BlockSpec((1, d), lambda r, f: (0, 0)),
                  modvec(scale_idx, 0), modvec(shift_idx, 0), modvec(gate_idx, 2),
                  pl.BlockSpec((1, d), lambda r, f: (0, 0)),
                  pl.BlockSpec((d, tf), lambda r, f: (0, wtile(r, f))),
                  pl.BlockSpec((d, tf), lambda r, f: (0, wtile(r, f))),
                  pl.BlockSpec((tf, d), lambda r, f: (wtile(r, f), 0))],
        out_specs=pl.BlockSpec((FFN_STRIP, d), lambda r, f: (strip(r - 2, f), 0)),
        out_shape=jax.ShapeDtypeStruct((t, d), F32),
        scratch_shapes=[pltpu.VMEM((2, tm, d), BF16),
                        pltpu.VMEM((tm, d), F32),
                        pltpu.VMEM((tm, d), BF16),
                        vec_scr(), vec_scr(), vec_scr(), vec_scr()],
        compiler_params=_params("arbitrary", "arbitrary", vmem=FFN_VMEM_LIMIT_BYTES),
        name="swiglu_ffn",
    )(x1, x1, g2, mod4, mod4, mod4, gf, w1, w3, w2)


def kernel(x, c, positions, norm1_g, norm2_g, normf_g, w_mod, b_mod, w_in, conv_dw_w, conv_dw_b,
           conv_ln_g, conv_ln_b, ret_gn_g, w_out, w_ffn1, w_ffn3, w_ffn2):
    batch, seq, d = x.shape
    depth = w_mod.shape[0]
    t = batch * seq
    conv_ch = conv_dw_w.shape[-1]
    ret_w = ret_gn_g.shape[-1]
    hd = ret_w // RET_HEADS

    half = hd // 2
    freqs = (ROPE_BASE ** (-jnp.arange(half, dtype=F32) / half)).reshape(1, half)
    pos_b = jnp.broadcast_to(positions.astype(F32).reshape(t, 1), (t, half))
    cos_t, sin_t = _rope_tables(pos_b, freqs)
    dec, xi_b, zeta_b, cd_b = _retention_tables(hd)

    c_pad = jnp.pad(c, ((0, SUBLANES - batch % SUBLANES), (0, 0))) if batch % SUBLANES else c
    x2 = x.reshape(t, d)
    row = lambda v: v.reshape(1, -1)

    for l in range(depth):
        mod = _modulation(c_pad, w_mod[l], row(b_mod[l]))
        mod4 = mod[:batch].reshape(batch, N_MOD, 1, d)

        h = _norm_mod(x2, row(norm1_g[l]), mod4, 1, 0, seq)
        p, conv_out = _inproj_conv(h, w_in[l].astype(BF16), conv_dw_w[l], row(conv_dw_b[l]),
                                   row(conv_ln_g[l]),
                                   row(conv_ln_b[l]), conv_ch, seq, tm=IN_PROJ_TM, tn=IN_PROJ_TN)
        ret_out = _retention_group(p, cos_t, sin_t, dec, xi_b, zeta_b, cd_b, row(ret_gn_g[l]),
                                   batch, seq, 0, hd)

        x2 = _outproj(conv_out, ret_out, w_out[l].astype(BF16), x2, mod4, 2, seq)

        x2 = _ffn(x2, row(norm2_g[l]), mod4, 4, 3, 5, row(normf_g), w_ffn1[l].astype(BF16),
                  w_ffn3[l].astype(BF16), w_ffn2[l].astype(BF16), seq,
                  final_norm=(l + 1 == depth), tm=FFN_TM)

    return x2.reshape(batch, seq, d)
```

```python
import functools
import math

import jax
import jax.numpy as jnp
from jax import lax
from jax.experimental import pallas as pl
from jax.experimental.pallas import tpu as pltpu

F32 = jnp.float32
BF16 = jnp.bfloat16

N_MOD = 6
CONV_WIDTH = 31
RET_HEADS = 8
ROPE_BASE = 10000.0
EPS = 1e-6
LN_EPS = 1e-5

LANES = 128
SUBLANES = 8
VMEM_LIMIT_BYTES = 56 * 1024 * 1024

RET_CHUNK = 256
CONV_HALO = 32
CONV_TILE_ROWS = 32
LN_ROWS = 16
IN_PROJ_TM = 1024
IN_PROJ_TN = 1024
IN_PROJ_STRIP = 128
IN_PROJ_DELAY_LINKS = 6
IN_PROJ_VMEM_LIMIT_BYTES = 60 * 1024 * 1024
FFN_TM = 1024
FFN_STRIP = 32
FFN_DELAY_LINKS = 16


def _params(*semantics, vmem=VMEM_LIMIT_BYTES):
    return pltpu.CompilerParams(dimension_semantics=semantics, vmem_limit_bytes=vmem)


def _silu(x):
    return x * jax.nn.sigmoid(x)


def _mod_kernel(c_ref, w_ref, b_ref, o_ref):
    ca = _silu(c_ref[...]).astype(BF16)
    o_ref[...] = jnp.dot(ca, w_ref[...].astype(BF16), preferred_element_type=F32) + b_ref[...]


def _modulation(c_pad, w_mod, b_mod, tn=512):
    rows, d = c_pad.shape
    n = w_mod.shape[1]
    return pl.pallas_call(
        _mod_kernel,
        grid=(n // tn,),
        in_specs=[pl.BlockSpec((rows, d), lambda j: (0, 0)),
                  pl.BlockSpec((d, tn), lambda j: (0, j)),
                  pl.BlockSpec((1, tn), lambda j: (0, j))],
        out_specs=pl.BlockSpec((rows, tn), lambda j: (0, j)),
        out_shape=jax.ShapeDtypeStruct((rows, n), F32),
        compiler_params=_params("arbitrary"),
        name="adaln_mod",
    )(c_pad, w_mod, b_mod)


def _zero_after(x):
    s = jnp.sum(x[0:SUBLANES, :], axis=1, keepdims=True)
    bits = lax.bitcast_convert_type(jnp.broadcast_to(s[0:1, :], (1, LANES)), jnp.uint32)
    sixteen = jnp.uint32(16)
    bits = lax.shift_right_logical(lax.shift_right_logical(bits, sixteen), sixteen)
    return lax.bitcast_convert_type(bits, F32)


def _conv_tile(win_ref, cw_ref, base, lanes, after=None):
    first = CONV_HALO - (CONV_WIDTH - 1)
    y = None
    for r in range(SUBLANES):
        off = first + r
        lo = base + (off // SUBLANES) * SUBLANES
        sub = off % SUBLANES
        span = CONV_TILE_ROWS + (SUBLANES if sub else 0)
        u = None
        for q in range((CONV_WIDTH - 1 - r) // SUBLANES + 1):
            tap = q * SUBLANES + r
            rows = slice(lo + q * SUBLANES, lo + q * SUBLANES + span)
            w_row = cw_ref[tap:tap + 1, lanes]
            if after is not None and tap == 0:
                w_row = w_row + after
            term = w_row * win_ref[rows, lanes]
            u = term if u is None else u + term
        u = u[sub:sub + CONV_TILE_ROWS, :]
        y = u if y is None else y + u
    return y


def _inproj_conv_kernel(x_ref, g_ref, scale_ref, shift_ref, w_ref, cw_ref, cb_ref, lg_ref, lb_ref,
                        p_ref, co_ref, h_scr, a_scr, win_scr, y_scr, gs_scr, sh_scr, sink_scr,
                        *, tm, ch, n_glu, n_strips, rps, blocks_per_seq):
    r = pl.program_id(0)
    j = pl.program_id(1)
    i = r - 1
    mm_here = r >= 1
    strip_rows = x_ref.shape[0]

    @pl.when(j == 0)
    def _():
        gs_scr[...] = jnp.broadcast_to(g_ref[...] * (1.0 + scale_ref[...]), gs_scr.shape)
        sh_scr[...] = jnp.broadcast_to(shift_ref[...], sh_scr.shape)

        @pl.when(r == 0)
        def _():
            sink_scr[...] = jnp.zeros(sink_scr.shape, F32)

        @pl.when(jnp.logical_and(mm_here, i % blocks_per_seq == 0))
        def _():
            a_scr[0:CONV_HALO, :] = jnp.zeros((CONV_HALO, ch), F32)

        @pl.when(jnp.logical_and(mm_here, i % blocks_per_seq != 0))
        def _():
            a_scr[0:CONV_HALO, :] = a_scr[tm:tm + CONV_HALO, :]

        @pl.when(mm_here)
        def _():
            a_scr[0:SUBLANES, 0:LANES] += sink_scr[...]

    def norm_strip(delayed):
        zero = None
        for k in range(strip_rows // LN_ROWS):
            x = x_ref[k * LN_ROWS:(k + 1) * LN_ROWS, :]
            inv = lax.rsqrt(jnp.mean(x * x, axis=-1, keepdims=True) + EPS)
            h = (x * inv * gs_scr[...] + sh_scr[...]).astype(h_scr.dtype)
            row0 = pl.multiple_of(j * strip_rows + k * LN_ROWS, LN_ROWS)
            h_scr[r % 2, pl.ds(row0, LN_ROWS), :] = h
            if delayed:
                z = _delayed_zero(h, IN_PROJ_DELAY_LINKS * (k + 1))
                zero = z if zero is None else zero + z
        if delayed:
            sink_scr[...] = zero

    def lhs():
        return h_scr[(r + 1) % 2]

    body = slice(CONV_HALO, CONV_HALO + tm)
    tn = w_ref.shape[1]
    n_val = n_glu // 2

    @pl.when(jnp.logical_and(jnp.logical_not(mm_here), j < n_strips))
    def _():
        norm_strip(delayed=False)

    half = tn // 2

    def tile_dot(c):
        return jnp.dot(lhs(), w_ref[:, c * half:(c + 1) * half], preferred_element_type=F32)

    @pl.when(jnp.logical_and(mm_here, j < n_val))
    def _():
        for c in range(2):
            cols = pl.ds(pl.multiple_of(j * tn + c * half, half), half)
            a_scr[body, cols] = tile_dot(c)
        norm_strip(delayed=True)

    @pl.when(jnp.logical_and(mm_here, jnp.logical_and(j >= n_val, j < n_glu)))
    def _():
        for c in range(2):
            cols = pl.ds(pl.multiple_of((j - n_val) * tn + c * half, half), half)
            a_scr[body, cols] = a_scr[body, cols] * jax.nn.sigmoid(tile_dot(c))
        norm_strip(delayed=True)

    def conv_step(with_strip):
        r0 = pl.multiple_of((j - n_glu) * rps, rps)
        win_scr[...] = a_scr[pl.ds(r0, rps + CONV_HALO), :]

        for c in range(2):
            p_ref[:, c * half:(c + 1) * half] = tile_dot(c).astype(p_ref.dtype)

        after = None
        for c in range(ch // LANES):
            lanes = slice(c * LANES, (c + 1) * LANES)
            for rt in range(rps // CONV_TILE_ROWS):
                base = rt * CONV_TILE_ROWS
                y = _conv_tile(win_scr, cw_ref, base, lanes, after) + cb_ref[:, lanes]
                y_scr[base:base + CONV_TILE_ROWS, lanes] = y
                after = _zero_after(y)

        for k in range(rps // LN_ROWS):
            y = y_scr[k * LN_ROWS:(k + 1) * LN_ROWS, :]
            mu = jnp.mean(y, axis=-1, keepdims=True)
            d = y - mu
            var = jnp.mean(d * d, axis=-1, keepdims=True)
            z = d * lax.rsqrt(var + LN_EPS) * lg_ref[...] + lb_ref[...]
            co_ref[k * LN_ROWS:(k + 1) * LN_ROWS, :] = _silu(z).astype(co_ref.dtype)

        if with_strip:
            norm_strip(delayed=True)

    @pl.when(jnp.logical_and(mm_here, jnp.logical_and(j >= n_glu, j < n_strips)))
    def _():
        conv_step(with_strip=True)

    @pl.when(jnp.logical_and(mm_here, j >= max(n_glu, n_strips)))
    def _():
        conv_step(with_strip=False)


def _inproj_conv(x2, g, mod4, scale_idx, shift_idx, w, dw_w, dw_b, ln_g, ln_b, ch, seq,
                 tm=1024, tn=1024):
    t, d = x2.shape
    n = w.shape[1]
    n_glu = 2 * ch // tn
    n_col = n // tn
    n_conv = n_col - n_glu
    rps = tm // n_conv
    n_blocks = t // tm
    n_strips = tm // IN_PROJ_STRIP
    per_b = seq // tm
    assert rps % CONV_TILE_ROWS == 0 and seq % tm == 0 and ch % tn == 0
    assert n_glu <= n_strips <= n_col
    kern = functools.partial(_inproj_conv_kernel, tm=tm, ch=ch, n_glu=n_glu, n_strips=n_strips,
                             rps=rps, blocks_per_seq=per_b)
    vec = lambda: pl.BlockSpec((1, ch), lambda r, j: (0, 0))
    blk = lambda r: jnp.minimum(r, n_blocks - 1)
    modvec = lambda idx: pl.BlockSpec((None, None, 1, d), lambda r, j: (blk(r) // per_b, idx, 0, 0))
    col = lambda r, j: jnp.where(r == 0, 0, jnp.maximum(j - n_glu, 0))
    return pl.pallas_call(
        kern,
        grid=(n_blocks + 1, n_col),
        in_specs=[pl.BlockSpec((IN_PROJ_STRIP, d),
                               lambda r, j: (jnp.where(r < n_blocks,
                                                       r * n_strips + jnp.minimum(j, n_strips - 1),
                                                       n_blocks * n_strips - 1), 0)),
                  pl.BlockSpec((1, d), lambda r, j: (0, 0)),
                  modvec(scale_idx), modvec(shift_idx),
                  pl.BlockSpec((d, tn), lambda r, j: (0, jnp.where(r == 0, 0, j))),
                  pl.BlockSpec((CONV_WIDTH, ch), lambda r, j: (0, 0)),
                  vec(), vec(), vec()],
        out_specs=[pl.BlockSpec((tm, tn), lambda r, j: (jnp.maximum(r - 1, 0), col(r, j))),
                   pl.BlockSpec((rps, ch),
                                lambda r, j: (jnp.maximum(r - 1, 0) * n_conv + col(r, j), 0))],
        out_shape=[jax.ShapeDtypeStruct((t, n - 2 * ch), BF16),
                   jax.ShapeDtypeStruct((t, ch), BF16)],
        scratch_shapes=[pltpu.VMEM((2, tm, d), BF16),
                        pltpu.VMEM((CONV_HALO + tm, ch), F32),
                        pltpu.VMEM((rps + CONV_HALO, ch), F32),
                        pltpu.VMEM((rps, ch), F32),
                        pltpu.VMEM((LN_ROWS, d), F32),
                        pltpu.VMEM((LN_ROWS, d), F32),
                        pltpu.VMEM((SUBLANES, LANES), F32)],
        compiler_params=_params("arbitrary", "arbitrary", vmem=IN_PROJ_VMEM_LIMIT_BYTES),
        name="in_proj_conv",
    )(x2, g, mod4, mod4, w, dw_w, dw_b, ln_g, ln_b)


def _rope_table_kernel(pos_ref, freq_ref, cos_ref, sin_ref):
    ang = pos_ref[...] * freq_ref[...]
    cos_ref[...] = jnp.cos(ang)
    sin_ref[...] = jnp.sin(ang)


def _rope_tables(pos_b, freqs, ts=512):
    t, half = pos_b.shape
    blk = lambda: pl.BlockSpec((ts, half), lambda i: (i, 0))
    return pl.pallas_call(
        _rope_table_kernel,
        grid=(t // ts,),
        in_specs=[blk(), pl.BlockSpec((1, half), lambda i: (0, 0))],
        out_specs=[blk(), blk()],
        out_shape=[jax.ShapeDtypeStruct((t, half), F32)] * 2,
        compiler_params=_params("arbitrary"),
        name="rope_tables",
    )(pos_b, freqs)


def _ret_kernel(q_ref, k_ref, v_ref, g_ref, cos_ref, sin_ref, dec_ref, xi_ref, zeta_ref, cd_ref,
                gn_ref, o_ref, state_ref, *, heads, hd, kscale):
    @pl.when(pl.program_id(1) == 0)
    def _():
        state_ref[...] = jnp.zeros(state_ref.shape, F32)

    half = hd // 2
    cos = cos_ref[...]
    sin = sin_ref[...]

    def rope(t_ref, lo):
        t1 = t_ref[:, lo:lo + half].astype(F32)
        t2 = t_ref[:, lo + half:lo + hd].astype(F32)
        return jnp.concatenate([t1 * cos - t2 * sin, t2 * cos + t1 * sin], axis=-1)

    for hh in range(heads):
        lo = hh * hd
        cols = slice(lo, lo + hd)
        qr = rope(q_ref, lo)
        kr = rope(k_ref, lo) * kscale
        v = v_ref[:, cols]

        scores = lax.dot_general(qr.astype(BF16), kr.astype(BF16), (((1,), (1,)), ((), ())),
                                 preferred_element_type=F32) * dec_ref[hh]
        inner = jnp.dot(scores.astype(BF16), v, preferred_element_type=F32)

        state = state_ref[hh]
        cross = jnp.dot((qr * xi_ref[hh]).astype(BF16), state.astype(BF16),
                        preferred_element_type=F32)
        kz_t = jnp.transpose(kr * zeta_ref[hh]).astype(BF16)
        state_ref[hh] = state * cd_ref[hh] + jnp.dot(kz_t, v, preferred_element_type=F32)

        out = inner + cross
        mu = jnp.mean(out, axis=-1, keepdims=True)
        d = out - mu
        var = jnp.mean(d * d, axis=-1, keepdims=True)
        y = d * lax.rsqrt(var + LN_EPS) * gn_ref[:, cols]
        o_ref[:, cols] = (_silu(g_ref[:, cols].astype(F32)) * y).astype(o_ref.dtype)


def _retention_group(p, cos_t, sin_t, dec, xi_b, zeta_b, cd_b, gn_g, batch, seq, col0, hd):
    t = p.shape[0]
    c = RET_CHUNK
    nc = seq // c
    h = RET_HEADS
    w = h * hd
    first = col0 // w
    row = lambda b, n: b * nc + n
    group_blk = lambda i: pl.BlockSpec((c, w), lambda b, n: (row(b, n), first + i))
    whole = lambda a: pl.BlockSpec(a.shape, lambda b, n: (0,) * a.ndim)
    rope_blk = lambda: pl.BlockSpec((c, hd // 2), lambda b, n: (row(b, n), 0))
    kern = functools.partial(_ret_kernel, heads=h, hd=hd, kscale=float(hd) ** -0.5)
    return pl.pallas_call(
        kern,
        grid=(batch, nc),
        in_specs=[group_blk(0), group_blk(1), group_blk(2), group_blk(3),
                  rope_blk(), rope_blk(),
                  whole(dec), whole(xi_b), whole(zeta_b), whole(cd_b), whole(gn_g)],
        out_specs=pl.BlockSpec((c, w), lambda b, n: (row(b, n), 0)),
        out_shape=jax.ShapeDtypeStruct((t, w), BF16),
        scratch_shapes=[pltpu.VMEM((h, hd, hd), F32)],
        compiler_params=_params("arbitrary", "arbitrary"),
        name="retention_group",
    )(p, p, p, p, cos_t, sin_t, dec, xi_b, zeta_b, cd_b, gn_g)


def _retention_tables(hd):
    c = RET_CHUNK
    heads = jnp.arange(RET_HEADS, dtype=F32)
    log_gamma = jnp.log(1.0 - 2.0 ** (-5.0 - heads))
    idx = jnp.arange(c)
    idx_f = idx.astype(F32)
    diff = (idx[:, None] - idx[None, :]).astype(F32)
    dec = jnp.where(diff[None] >= 0,
                    jnp.exp(jnp.maximum(diff, 0.0)[None] * log_gamma[:, None, None]), 0.0)
    xi = jnp.exp((idx_f + 1.0)[None] * log_gamma[:, None])
    zeta = jnp.exp((c - 1.0 - idx_f)[None] * log_gamma[:, None])
    cd = jnp.exp(c * log_gamma)
    bcast = lambda a: jnp.broadcast_to(a[:, :, None], (RET_HEADS, c, hd))
    return dec, bcast(xi), bcast(zeta), jnp.broadcast_to(cd[:, None, None], (RET_HEADS, 1, hd))


def _outproj_kernel(a1_ref, a2_ref, w1_ref, w2_ref, x_ref, gate_ref, o_ref):
    mix = (jnp.dot(a1_ref[...], w1_ref[...], preferred_element_type=F32)
           + jnp.dot(a2_ref[...], w2_ref[...], preferred_element_type=F32))
    o_ref[...] = x_ref[...] + gate_ref[...] * mix


def _outproj(a1, a2, w, x2, mod4, gate_idx, seq, tm=1024, tn=512):
    m, k1 = a1.shape
    k2 = a2.shape[1]
    n = w.shape[1]
    per_b = seq // tm
    return pl.pallas_call(
        _outproj_kernel,
        grid=(m // tm, n // tn),
        in_specs=[pl.BlockSpec((tm, k1), lambda i, j: (i, 0)),
                  pl.BlockSpec((tm, k2), lambda i, j: (i, 0)),
                  pl.BlockSpec((k1, tn), lambda i, j: (0, j)),
                  pl.BlockSpec((k2, tn), lambda i, j: (k1 // k2, j)),
                  pl.BlockSpec((tm, tn), lambda i, j: (i, j)),
                  pl.BlockSpec((None, None, 1, tn), lambda i, j: (i // per_b, gate_idx, 0, j))],
        out_specs=pl.BlockSpec((tm, tn), lambda i, j: (i, j)),
        out_shape=jax.ShapeDtypeStruct((m, n), F32),
        compiler_params=_params("arbitrary", "arbitrary"),
        name="out_proj",
    )(a1, a2, w, w, x2, mod4)


def _delayed_zero(x, links):
    s0 = x[0:SUBLANES, 0:LANES].astype(F32)
    s = s0
    for _ in range(links):
        s = (pltpu.roll(s, 1, axis=1) + s0) * 0.5
    bits = lax.bitcast_convert_type(s, jnp.uint32)
    sixteen = jnp.uint32(16)
    bits = lax.shift_right_logical(lax.shift_right_logical(bits, sixteen), sixteen)
    return lax.bitcast_convert_type(bits, F32)


def _ffn_kernel(x1_ref, g_ref, scale_ref, shift_ref, w1_ref, w3_ref, w2_ref, o_ref,
                h_scr, acc_ref, gs_scr, sh_scr, *, n_blocks, n_strips):
    r = pl.program_id(0)
    f = pl.program_id(1)
    norm_here = jnp.logical_and(r < n_blocks, f < n_strips)
    mm_here = r >= 1

    @pl.when(jnp.logical_and(r < n_blocks, f == 0))
    def _():
        gs_scr[...] = jnp.broadcast_to(g_ref[...] * (1.0 + scale_ref[...]), gs_scr.shape)
        sh_scr[...] = jnp.broadcast_to(shift_ref[...], sh_scr.shape)

    @pl.when(jnp.logical_and(mm_here, f == 0))
    def _():
        acc_ref[...] = jnp.zeros(acc_ref.shape, F32)

    def norm_strip():
        outs = []
        for k in range(FFN_STRIP // LN_ROWS):
            x = x1_ref[k * LN_ROWS:(k + 1) * LN_ROWS, :]
            inv = lax.rsqrt(jnp.mean(x * x, axis=-1, keepdims=True) + EPS)
            h = (x * inv * gs_scr[...] + sh_scr[...]).astype(h_scr.dtype)
            row0 = pl.multiple_of(f * FFN_STRIP + k * LN_ROWS, LN_ROWS)
            h_scr[r % 2, pl.ds(row0, LN_ROWS), :] = h
            outs.append(h)
        return outs

    def matmuls():
        h = h_scr[(r + 1) % 2]
        a = jnp.dot(h, w1_ref[...], preferred_element_type=F32)
        b = jnp.dot(h, w3_ref[...], preferred_element_type=F32)
        u = (_silu(a) * b).astype(BF16)
        acc_ref[...] += jnp.dot(u, w2_ref[...], preferred_element_type=F32)

    @pl.when(jnp.logical_and(norm_here, jnp.logical_not(mm_here)))
    def _():
        norm_strip()

    @pl.when(jnp.logical_and(norm_here, mm_here))
    def _():
        matmuls()
        zero = None
        for k, h in enumerate(norm_strip()):
            z = _delayed_zero(h, FFN_DELAY_LINKS * (k + 1))
            zero = z if zero is None else zero + z
        acc_ref[0:SUBLANES, 0:LANES] += zero

    @pl.when(jnp.logical_and(jnp.logical_not(norm_here), mm_here))
    def _():
        matmuls()

    @pl.when(jnp.logical_and(mm_here, f == pl.num_programs(1) - 1))
    def _():
        o_ref[...] = acc_ref[...].astype(o_ref.dtype)


def _ffn(x1, g, mod4, scale_idx, shift_idx, w1, w3, w2, seq, tm=1024, tf=256):
    t, d = x1.shape
    nf = w1.shape[1] // tf
    n_blocks = t // tm
    n_strips = tm // FFN_STRIP
    per_b = seq // tm
    assert nf >= n_strips and seq % tm == 0
    last_strip = n_blocks * n_strips - 1
    blk = lambda r: jnp.minimum(r, n_blocks - 1)
    modvec = lambda idx: pl.BlockSpec((None, None, 1, d), lambda r, f: (blk(r) // per_b, idx, 0, 0))
    wcol = lambda r, f: (0, jnp.where(r == 0, 0, f))
    kern = functools.partial(_ffn_kernel, n_blocks=n_blocks, n_strips=n_strips)
    return pl.pallas_call(
        kern,
        grid=(n_blocks + 1, nf),
        in_specs=[pl.BlockSpec((FFN_STRIP, d),
                               lambda r, f: (jnp.where(r < n_blocks,
                                                       r * n_strips + jnp.minimum(f, n_strips - 1),
                                                       last_strip), 0)),
                  pl.BlockSpec((1, d), lambda r, f: (0, 0)),
                  modvec(scale_idx), modvec(shift_idx),
                  pl.BlockSpec((d, tf), wcol),
                  pl.BlockSpec((d, tf), wcol),
                  pl.BlockSpec((tf, d), lambda r, f: (jnp.where(r == 0, 0, f), 0))],
        out_specs=pl.BlockSpec((tm, d), lambda r, f: (jnp.maximum(r - 1, 0), 0),
                               pipeline_mode=pl.Buffered(1)),
        out_shape=jax.ShapeDtypeStruct((t, d), BF16),
        scratch_shapes=[pltpu.VMEM((2, tm, d), BF16),
                        pltpu.VMEM((tm, d), F32),
                        pltpu.VMEM((LN_ROWS, d), F32),
                        pltpu.VMEM((LN_ROWS, d), F32)],
        compiler_params=_params("arbitrary", "arbitrary"),
        name="swiglu_ffn",
    )(x1, g, mod4, mod4, w1, w3, w2)


def _residual_kernel(x_ref, ff_ref, gate_ref, g_ref, o_ref, gate_scr, g_scr, *, final_norm):
    strip = gate_scr.shape
    gate_scr[...] = jnp.broadcast_to(gate_ref[...], strip)
    g_scr[...] = jnp.broadcast_to(g_ref[...], strip)
    for r in range(x_ref.shape[0] // LN_ROWS):
        rows = slice(r * LN_ROWS, (r + 1) * LN_ROWS)
        y = x_ref[rows, :] + gate_scr[...] * ff_ref[rows, :].astype(F32)
        if final_norm:
            inv = lax.rsqrt(jnp.mean(y * y, axis=-1, keepdims=True) + EPS)
            y = y * inv * g_scr[...]
        o_ref[rows, :] = y


def _residual(x1, ff, mod4, gate_idx, g, seq, final_norm, ts=256):
    t, d = x1.shape
    per_b = seq // ts
    return pl.pallas_call(
        functools.partial(_residual_kernel, final_norm=final_norm),
        grid=(t // ts,),
        in_specs=[pl.BlockSpec((ts, d), lambda i: (i, 0)),
                  pl.BlockSpec((ts, d), lambda i: (i, 0)),
                  pl.BlockSpec((None, None, 1, d), lambda i: (i // per_b, gate_idx, 0, 0)),
                  pl.BlockSpec((1, d), lambda i: (0, 0))],
        out_specs=pl.BlockSpec((ts, d), lambda i: (i, 0)),
        out_shape=jax.ShapeDtypeStruct((t, d), F32),
        scratch_shapes=[pltpu.VMEM((LN_ROWS, d), F32), pltpu.VMEM((LN_ROWS, d), F32)],
        compiler_params=_params("arbitrary"),
        name="final_norm",
    )(x1, ff, mod4, g)


def kernel(x, c, positions, norm1_g, norm2_g, normf_g, w_mod, b_mod, w_in, conv_dw_w, conv_dw_b,
           conv_ln_g, conv_ln_b, ret_gn_g, w_out, w_ffn1, w_ffn3, w_ffn2):
    batch, seq, d = x.shape
    depth = w_mod.shape[0]
    t = batch * seq
    conv_ch = conv_dw_w.shape[-1]
    ret_w = ret_gn_g.shape[-1]
    hd = ret_w // RET_HEADS

    half = hd // 2
    freqs = (ROPE_BASE ** (-jnp.arange(half, dtype=F32) / half)).reshape(1, half)
    pos_b = jnp.broadcast_to(positions.astype(F32).reshape(t, 1), (t, half))
    cos_t, sin_t = _rope_tables(pos_b, freqs)
    dec, xi_b, zeta_b, cd_b = _retention_tables(hd)

    c_pad = jnp.pad(c, ((0, SUBLANES - batch % SUBLANES), (0, 0))) if batch % SUBLANES else c
    x2 = x.reshape(t, d)
    row = lambda v: v.reshape(1, -1)

    for l in range(depth):
        mod = _modulation(c_pad, w_mod[l], row(b_mod[l]))
        mod4 = mod[:batch].reshape(batch, N_MOD, 1, d)

        p, conv_out = _inproj_conv(x2, row(norm1_g[l]), mod4, 1, 0, w_in[l].astype(BF16),
                                   conv_dw_w[l], row(conv_dw_b[l]), row(conv_ln_g[l]),
                                   row(conv_ln_b[l]), conv_ch, seq, tm=IN_PROJ_TM, tn=IN_PROJ_TN)
        ret_out = _retention_group(p, cos_t, sin_t, dec, xi_b, zeta_b, cd_b, row(ret_gn_g[l]),
                                   batch, seq, 0, hd)

        x2 = _outproj(conv_out, ret_out, w_out[l].astype(BF16), x2, mod4, 2, seq)

        ff = _ffn(x2, row(norm2_g[l]), mod4, 4, 3, w_ffn1[l].astype(BF16), w_ffn3[l].astype(BF16),
                  w_ffn2[l].astype(BF16), seq, tm=FFN_TM)
        x2 = _residual(x2, ff, mod4, 5, row(normf_g), seq, final_norm=(l + 1 == depth))

    return x2.reshape(batch, seq, d)
```

```python
import functools
import math

import jax
import jax.numpy as jnp
from jax import lax
from jax.experimental import pallas as pl
from jax.experimental.pallas import tpu as pltpu

F32 = jnp.float32
BF16 = jnp.bfloat16

N_MOD = 6
CONV_WIDTH = 31
RET_HEADS = 8
ROPE_BASE = 10000.0
EPS = 1e-6
LN_EPS = 1e-5

LANES = 128
SUBLANES = 8
VMEM_LIMIT_BYTES = 56 * 1024 * 1024

RET_CHUNK = 256
CONV_HALO = 32
CONV_TILE_ROWS = 32
LN_ROWS = 16
IN_PROJ_TM = 1024
IN_PROJ_TN = 1024
IN_PROJ_STRIP = 128
FFN_TM = 1024
FFN_STRIP = 32


def _params(*semantics, vmem=VMEM_LIMIT_BYTES):
    return pltpu.CompilerParams(dimension_semantics=semantics, vmem_limit_bytes=vmem)


def _silu(x):
    return x * jax.nn.sigmoid(x)


def _mod_kernel(c_ref, w_ref, b_ref, o_ref):
    ca = _silu(c_ref[...]).astype(BF16)
    o_ref[...] = jnp.dot(ca, w_ref[...].astype(BF16), preferred_element_type=F32) + b_ref[...]


def _modulation(c_pad, w_mod, b_mod, tn=512):
    rows, d = c_pad.shape
    n = w_mod.shape[1]
    return pl.pallas_call(
        _mod_kernel,
        grid=(n // tn,),
        in_specs=[pl.BlockSpec((rows, d), lambda j: (0, 0)),
                  pl.BlockSpec((d, tn), lambda j: (0, j)),
                  pl.BlockSpec((1, tn), lambda j: (0, j))],
        out_specs=pl.BlockSpec((rows, tn), lambda j: (0, j)),
        out_shape=jax.ShapeDtypeStruct((rows, n), F32),
        compiler_params=_params("arbitrary"),
        name="adaln_mod",
    )(c_pad, w_mod, b_mod)


def _zero_after(x):
    s = jnp.sum(x[0:SUBLANES, :], axis=1, keepdims=True)
    bits = lax.bitcast_convert_type(jnp.broadcast_to(s[0:1, :], (1, LANES)), jnp.uint32)
    sixteen = jnp.uint32(16)
    bits = lax.shift_right_logical(lax.shift_right_logical(bits, sixteen), sixteen)
    return lax.bitcast_convert_type(bits, F32)


def _conv_tile(win_ref, cw_ref, base, lanes, after=None):
    first = CONV_HALO - (CONV_WIDTH - 1)
    y = None
    for r in range(SUBLANES):
        off = first + r
        lo = base + (off // SUBLANES) * SUBLANES
        sub = off % SUBLANES
        span = CONV_TILE_ROWS + (SUBLANES if sub else 0)
        u = None
        for q in range((CONV_WIDTH - 1 - r) // SUBLANES + 1):
            tap = q * SUBLANES + r
            rows = slice(lo + q * SUBLANES, lo + q * SUBLANES + span)
            w_row = cw_ref[tap:tap + 1, lanes]
            if after is not None and tap == 0:
                w_row = w_row + after
            term = w_row * win_ref[rows, lanes]
            u = term if u is None else u + term
        u = u[sub:sub + CONV_TILE_ROWS, :]
        y = u if y is None else y + u
    return y


def _inproj_conv_kernel(x_ref, g_ref, scale_ref, shift_ref, w_ref, cw_ref, cb_ref, lg_ref, lb_ref,
                        p_ref, co_ref, h_scr, a_scr, win_scr, y_scr, gs_scr, sh_scr,
                        *, tm, ch, n_glu, n_strips, rps, blocks_per_seq):
    r = pl.program_id(0)
    j = pl.program_id(1)
    i = r - 1
    mm_here = r >= 1
    strip_rows = x_ref.shape[0]

    @pl.when(j == 0)
    def _():
        gs_scr[...] = jnp.broadcast_to(g_ref[...] * (1.0 + scale_ref[...]), gs_scr.shape)
        sh_scr[...] = jnp.broadcast_to(shift_ref[...], sh_scr.shape)

        @pl.when(jnp.logical_and(mm_here, i % blocks_per_seq == 0))
        def _():
            a_scr[0:CONV_HALO, :] = jnp.zeros((CONV_HALO, ch), F32)

        @pl.when(jnp.logical_and(mm_here, i % blocks_per_seq != 0))
        def _():
            a_scr[0:CONV_HALO, :] = a_scr[tm:tm + CONV_HALO, :]

    def norm_strip():
        for k in range(strip_rows // LN_ROWS):
            x = x_ref[k * LN_ROWS:(k + 1) * LN_ROWS, :]
            inv = lax.rsqrt(jnp.mean(x * x, axis=-1, keepdims=True) + EPS)
            h = (x * inv * gs_scr[...] + sh_scr[...]).astype(h_scr.dtype)
            row0 = pl.multiple_of(j * strip_rows + k * LN_ROWS, LN_ROWS)
            h_scr[r % 2, pl.ds(row0, LN_ROWS), :] = h

    def lhs():
        return h_scr[(r + 1) % 2]

    body = slice(CONV_HALO, CONV_HALO + tm)
    tn = w_ref.shape[1]
    n_val = n_glu // 2

    @pl.when(jnp.logical_and(jnp.logical_not(mm_here), j < n_strips))
    def _():
        norm_strip()

    half = tn // 2

    def tile_dot(c):
        return jnp.dot(lhs(), w_ref[:, c * half:(c + 1) * half], preferred_element_type=F32)

    @pl.when(jnp.logical_and(mm_here, j < n_val))
    def _():
        for c in range(2):
            cols = pl.ds(pl.multiple_of(j * tn + c * half, half), half)
            a_scr[body, cols] = tile_dot(c)
        norm_strip()

    @pl.when(jnp.logical_and(mm_here, jnp.logical_and(j >= n_val, j < n_glu)))
    def _():
        for c in range(2):
            cols = pl.ds(pl.multiple_of((j - n_val) * tn + c * half, half), half)
            a_scr[body, cols] = a_scr[body, cols] * jax.nn.sigmoid(tile_dot(c))
        norm_strip()

    def conv_step(with_strip):
        r0 = pl.multiple_of((j - n_glu) * rps, rps)
        win_scr[...] = a_scr[pl.ds(r0, rps + CONV_HALO), :]

        for c in range(2):
            p_ref[:, c * half:(c + 1) * half] = tile_dot(c).astype(p_ref.dtype)

        after = None
        for c in range(ch // LANES):
            lanes = slice(c * LANES, (c + 1) * LANES)
            for rt in range(rps // CONV_TILE_ROWS):
                base = rt * CONV_TILE_ROWS
                y = _conv_tile(win_scr, cw_ref, base, lanes, after) + cb_ref[:, lanes]
                y_scr[base:base + CONV_TILE_ROWS, lanes] = y
                after = _zero_after(y)

        for k in range(rps // LN_ROWS):
            y = y_scr[k * LN_ROWS:(k + 1) * LN_ROWS, :]
            mu = jnp.mean(y, axis=-1, keepdims=True)
            d = y - mu
            var = jnp.mean(d * d, axis=-1, keepdims=True)
            z = d * lax.rsqrt(var + LN_EPS) * lg_ref[...] + lb_ref[...]
            co_ref[k * LN_ROWS:(k + 1) * LN_ROWS, :] = _silu(z).astype(co_ref.dtype)

        if with_strip:
            norm_strip()

    @pl.when(jnp.logical_and(mm_here, jnp.logical_and(j >= n_glu, j < n_strips)))
    def _():
        conv_step(with_strip=True)

    @pl.when(jnp.logical_and(mm_here, j >= max(n_glu, n_strips)))
    def _():
        conv_step(with_strip=False)


def _inproj_conv(x2, g, mod4, scale_idx, shift_idx, w, dw_w, dw_b, ln_g, ln_b, ch, seq,
                 tm=1024, tn=1024):
    t, d = x2.shape
    n = w.shape[1]
    n_glu = 2 * ch // tn
    n_col = n // tn
    n_conv = n_col - n_glu
    rps = tm // n_conv
    n_blocks = t // tm
    n_strips = tm // IN_PROJ_STRIP
    per_b = seq // tm
    assert rps % CONV_TILE_ROWS == 0 and seq % tm == 0 and ch % tn == 0
    assert n_glu <= n_strips <= n_col
    kern = functools.partial(_inproj_conv_kernel, tm=tm, ch=ch, n_glu=n_glu, n_strips=n_strips,
                             rps=rps, blocks_per_seq=per_b)
    vec = lambda: pl.BlockSpec((1, ch), lambda r, j: (0, 0))
    blk = lambda r: jnp.minimum(r, n_blocks - 1)
    modvec = lambda idx: pl.BlockSpec((None, None, 1, d), lambda r, j: (blk(r) // per_b, idx, 0, 0))
    col = lambda r, j: jnp.where(r == 0, 0, jnp.maximum(j - n_glu, 0))
    return pl.pallas_call(
        kern,
        grid=(n_blocks + 1, n_col),
        in_specs=[pl.BlockSpec((IN_PROJ_STRIP, d),
                               lambda r, j: (jnp.where(r < n_blocks,
                                                       r * n_strips + jnp.minimum(j, n_strips - 1),
                                                       n_blocks * n_strips - 1), 0)),
                  pl.BlockSpec((1, d), lambda r, j: (0, 0)),
                  modvec(scale_idx), modvec(shift_idx),
                  pl.BlockSpec((d, tn), lambda r, j: (0, jnp.where(r == 0, 0, j))),
                  pl.BlockSpec((CONV_WIDTH, ch), lambda r, j: (0, 0)),
                  vec(), vec(), vec()],
        out_specs=[pl.BlockSpec((tm, tn), lambda r, j: (jnp.maximum(r - 1, 0), col(r, j))),
                   pl.BlockSpec((rps, ch),
                                lambda r, j: (jnp.maximum(r - 1, 0) * n_conv + col(r, j), 0))],
        out_shape=[jax.ShapeDtypeStruct((t, n - 2 * ch), BF16),
                   jax.ShapeDtypeStruct((t, ch), BF16)],
        scratch_shapes=[pltpu.VMEM((2, tm, d), BF16),
                        pltpu.VMEM((CONV_HALO + tm, ch), F32),
                        pltpu.VMEM((rps + CONV_HALO, ch), F32),
                        pltpu.VMEM((rps, ch), F32),
                        pltpu.VMEM((LN_ROWS, d), F32),
                        pltpu.VMEM((LN_ROWS, d), F32)],
        compiler_params=_params("arbitrary", "arbitrary"),
        name="in_proj_conv",
    )(x2, g, mod4, mod4, w, dw_w, dw_b, ln_g, ln_b)


def _rope_table_kernel(pos_ref, freq_ref, cos_ref, sin_ref):
    ang = pos_ref[...] * freq_ref[...]
    cos_ref[...] = jnp.cos(ang)
    sin_ref[...] = jnp.sin(ang)


def _rope_tables(pos_b, freqs, ts=512):
    t, half = pos_b.shape
    blk = lambda: pl.BlockSpec((ts, half), lambda i: (i, 0))
    return pl.pallas_call(
        _rope_table_kernel,
        grid=(t // ts,),
        in_specs=[blk(), pl.BlockSpec((1, half), lambda i: (0, 0))],
        out_specs=[blk(), blk()],
        out_shape=[jax.ShapeDtypeStruct((t, half), F32)] * 2,
        compiler_params=_params("arbitrary"),
        name="rope_tables",
    )(pos_b, freqs)


def _ret_kernel(q_ref, k_ref, v_ref, g_ref, cos_ref, sin_ref, dec_ref, xi_ref, zeta_ref, cd_ref,
                gn_ref, o_ref, state_ref, *, heads, hd, kscale):
    @pl.when(pl.program_id(1) == 0)
    def _():
        state_ref[...] = jnp.zeros(state_ref.shape, F32)

    half = hd // 2
    cos = cos_ref[...]
    sin = sin_ref[...]

    def rope(t_ref, lo):
        t1 = t_ref[:, lo:lo + half].astype(F32)
        t2 = t_ref[:, lo + half:lo + hd].astype(F32)
        return jnp.concatenate([t1 * cos - t2 * sin, t2 * cos + t1 * sin], axis=-1)

    for hh in range(heads):
        lo = hh * hd
        cols = slice(lo, lo + hd)
        qr = rope(q_ref, lo)
        kr = rope(k_ref, lo) * kscale
        v = v_ref[:, cols]

        scores = lax.dot_general(qr.astype(BF16), kr.astype(BF16), (((1,), (1,)), ((), ())),
                                 preferred_element_type=F32) * dec_ref[hh]
        inner = jnp.dot(scores.astype(BF16), v, preferred_element_type=F32)

        state = state_ref[hh]
        cross = jnp.dot((qr * xi_ref[hh]).astype(BF16), state.astype(BF16),
                        preferred_element_type=F32)
        kz_t = jnp.transpose(kr * zeta_ref[hh]).astype(BF16)
        state_ref[hh] = state * cd_ref[hh] + jnp.dot(kz_t, v, preferred_element_type=F32)

        out = inner + cross
        mu = jnp.mean(out, axis=-1, keepdims=True)
        d = out - mu
        var = jnp.mean(d * d, axis=-1, keepdims=True)
        y = d * lax.rsqrt(var + LN_EPS) * gn_ref[:, cols]
        o_ref[:, cols] = (_silu(g_ref[:, cols].astype(F32)) * y).astype(o_ref.dtype)


def _retention_group(p, cos_t, sin_t, dec, xi_b, zeta_b, cd_b, gn_g, batch, seq, col0, hd):
    t = p.shape[0]
    c = RET_CHUNK
    nc = seq // c
    h = RET_HEADS
    w = h * hd
    first = col0 // w
    row = lambda b, n: b * nc + n
    group_blk = lambda i: pl.BlockSpec((c, w), lambda b, n: (row(b, n), first + i))
    whole = lambda a: pl.BlockSpec(a.shape, lambda b, n: (0,) * a.ndim)
    rope_blk = lambda: pl.BlockSpec((c, hd // 2), lambda b, n: (row(b, n), 0))
    kern = functools.partial(_ret_kernel, heads=h, hd=hd, kscale=float(hd) ** -0.5)
    return pl.pallas_call(
        kern,
        grid=(batch, nc),
        in_specs=[group_blk(0), group_blk(1), group_blk(2), group_blk(3),
                  rope_blk(), rope_blk(),
                  whole(dec), whole(xi_b), whole(zeta_b), whole(cd_b), whole(gn_g)],
        out_specs=pl.BlockSpec((c, w), lambda b, n: (row(b, n), 0)),
        out_shape=jax.ShapeDtypeStruct((t, w), BF16),
        scratch_shapes=[pltpu.VMEM((h, hd, hd), F32)],
        compiler_params=_params("arbitrary", "arbitrary"),
        name="retention_group",
    )(p, p, p, p, cos_t, sin_t, dec, xi_b, zeta_b, cd_b, gn_g)


def _retention_tables(hd):
    c = RET_CHUNK
    heads = jnp.arange(RET_HEADS, dtype=F32)
    log_gamma = jnp.log(1.0 - 2.0 ** (-5.0 - heads))
    idx = jnp.arange(c)
    idx_f = idx.astype(F32)
    diff = (idx[:, None] - idx[None, :]).astype(F32)
    dec = jnp.where(diff[None] >= 0,
                    jnp.exp(jnp.maximum(diff, 0.0)[None] * log_gamma[:, None, None]), 0.0)
    xi = jnp.exp((idx_f + 1.0)[None] * log_gamma[:, None])
    zeta = jnp.exp((c - 1.0 - idx_f)[None] * log_gamma[:, None])
    cd = jnp.exp(c * log_gamma)
    bcast = lambda a: jnp.broadcast_to(a[:, :, None], (RET_HEADS, c, hd))
    return dec, bcast(xi), bcast(zeta), jnp.broadcast_to(cd[:, None, None], (RET_HEADS, 1, hd))


def _outproj_kernel(a1_ref, a2_ref, w1_ref, w2_ref, x_ref, gate_ref, o_ref):
    mix = (jnp.dot(a1_ref[...], w1_ref[...], preferred_element_type=F32)
           + jnp.dot(a2_ref[...], w2_ref[...], preferred_element_type=F32))
    o_ref[...] = x_ref[...] + gate_ref[...] * mix


def _outproj(a1, a2, w, x2, mod4, gate_idx, seq, tm=1024, tn=512):
    m, k1 = a1.shape
    k2 = a2.shape[1]
    n = w.shape[1]
    per_b = seq // tm
    return pl.pallas_call(
        _outproj_kernel,
        grid=(m // tm, n // tn),
        in_specs=[pl.BlockSpec((tm, k1), lambda i, j: (i, 0)),
                  pl.BlockSpec((tm, k2), lambda i, j: (i, 0)),
                  pl.BlockSpec((k1, tn), lambda i, j: (0, j)),
                  pl.BlockSpec((k2, tn), lambda i, j: (k1 // k2, j)),
                  pl.BlockSpec((tm, tn), lambda i, j: (i, j)),
                  pl.BlockSpec((None, None, 1, tn), lambda i, j: (i // per_b, gate_idx, 0, j))],
        out_specs=pl.BlockSpec((tm, tn), lambda i, j: (i, j)),
        out_shape=jax.ShapeDtypeStruct((m, n), F32),
        compiler_params=_params("arbitrary", "arbitrary"),
        name="out_proj",
    )(a1, a2, w, w, x2, mod4)


def _ffn_kernel(x1_ref, g_ref, scale_ref, shift_ref, w1_ref, w3_ref, w2_ref, o_ref,
                h_scr, acc_ref, gs_scr, sh_scr, *, n_blocks, n_strips):
    r = pl.program_id(0)
    f = pl.program_id(1)
    norm_here = jnp.logical_and(r < n_blocks, f < n_strips)
    mm_here = r >= 1

    @pl.when(jnp.logical_and(r < n_blocks, f == 0))
    def _():
        gs_scr[...] = jnp.broadcast_to(g_ref[...] * (1.0 + scale_ref[...]), gs_scr.shape)
        sh_scr[...] = jnp.broadcast_to(shift_ref[...], sh_scr.shape)

    @pl.when(jnp.logical_and(mm_here, f == 0))
    def _():
        acc_ref[...] = jnp.zeros(acc_ref.shape, F32)

    def norm_strip():
        for k in range(FFN_STRIP // LN_ROWS):
            x = x1_ref[k * LN_ROWS:(k + 1) * LN_ROWS, :]
            inv = lax.rsqrt(jnp.mean(x * x, axis=-1, keepdims=True) + EPS)
            h = (x * inv * gs_scr[...] + sh_scr[...]).astype(h_scr.dtype)
            row0 = pl.multiple_of(f * FFN_STRIP + k * LN_ROWS, LN_ROWS)
            h_scr[r % 2, pl.ds(row0, LN_ROWS), :] = h

    def matmuls():
        h = h_scr[(r + 1) % 2]
        a = jnp.dot(h, w1_ref[...], preferred_element_type=F32)
        b = jnp.dot(h, w3_ref[...], preferred_element_type=F32)
        u = (_silu(a) * b).astype(BF16)
        acc_ref[...] += jnp.dot(u, w2_ref[...], preferred_element_type=F32)

    @pl.when(jnp.logical_and(norm_here, jnp.logical_not(mm_here)))
    def _():
        norm_strip()

    @pl.when(jnp.logical_and(norm_here, mm_here))
    def _():
        matmuls()
        norm_strip()

    @pl.when(jnp.logical_and(jnp.logical_not(norm_here), mm_here))
    def _():
        matmuls()

    @pl.when(jnp.logical_and(mm_here, f == pl.num_programs(1) - 1))
    def _():
        o_ref[...] = acc_ref[...].astype(o_ref.dtype)


def _ffn(x1, g, mod4, scale_idx, shift_idx, w1, w3, w2, seq, tm=1024, tf=256):
    t, d = x1.shape
    nf = w1.shape[1] // tf
    n_blocks = t // tm
    n_strips = tm // FFN_STRIP
    per_b = seq // tm
    assert nf >= n_strips and seq % tm == 0
    last_strip = n_blocks * n_strips - 1
    blk = lambda r: jnp.minimum(r, n_blocks - 1)
    modvec = lambda idx: pl.BlockSpec((None, None, 1, d), lambda r, f: (blk(r) // per_b, idx, 0, 0))
    wcol = lambda r, f: (0, jnp.where(r == 0, 0, f))
    kern = functools.partial(_ffn_kernel, n_blocks=n_blocks, n_strips=n_strips)
    return pl.pallas_call(
        kern,
        grid=(n_blocks + 1, nf),
        in_specs=[pl.BlockSpec((FFN_STRIP, d),
                               lambda r, f: (jnp.where(r < n_blocks,
                                                       r * n_strips + jnp.minimum(f, n_strips - 1),
                                                       last_strip), 0)),
                  pl.BlockSpec((1, d), lambda r, f: (0, 0)),
                  modvec(scale_idx), modvec(shift_idx),
                  pl.BlockSpec((d, tf), wcol),
                  pl.BlockSpec((d, tf), wcol),
                  pl.BlockSpec((tf, d), lambda r, f: (jnp.where(r == 0, 0, f), 0))],
        out_specs=pl.BlockSpec((tm, d), lambda r, f: (jnp.maximum(r - 1, 0), 0),
                               pipeline_mode=pl.Buffered(1)),
        out_shape=jax.ShapeDtypeStruct((t, d), BF16),
        scratch_shapes=[pltpu.VMEM((2, tm, d), BF16),
                        pltpu.VMEM((tm, d), F32),
                        pltpu.VMEM((LN_ROWS, d), F32),
                        pltpu.VMEM((LN_ROWS, d), F32)],
        compiler_params=_params("arbitrary", "arbitrary"),
        name="swiglu_ffn",
    )(x1, g, mod4, mod4, w1, w3, w2)


def _residual_kernel(x_ref, ff_ref, gate_ref, g_ref, o_ref, gate_scr, g_scr, *, final_norm):
    strip = gate_scr.shape
    gate_scr[...] = jnp.broadcast_to(gate_ref[...], strip)
    g_scr[...] = jnp.broadcast_to(g_ref[...], strip)
    for r in range(x_ref.shape[0] // LN_ROWS):
        rows = slice(r * LN_ROWS, (r + 1) * LN_ROWS)
        y = x_ref[rows, :] + gate_scr[...] * ff_ref[rows, :].astype(F32)
        if final_norm:
            inv = lax.rsqrt(jnp.mean(y * y, axis=-1, keepdims=True) + EPS)
            y = y * inv * g_scr[...]
        o_ref[rows, :] = y


def _residual(x1, ff, mod4, gate_idx, g, seq, final_norm, ts=256):
    t, d = x1.shape
    per_b = seq // ts
    return pl.pallas_call(
        functools.partial(_residual_kernel, final_norm=final_norm),
        grid=(t // ts,),
        in_specs=[pl.BlockSpec((ts, d), lambda i: (i, 0)),
                  pl.BlockSpec((ts, d), lambda i: (i, 0)),
                  pl.BlockSpec((None, None, 1, d), lambda i: (i // per_b, gate_idx, 0, 0)),
                  pl.BlockSpec((1, d), lambda i: (0, 0))],
        out_specs=pl.BlockSpec((ts, d), lambda i: (i, 0)),
        out_shape=jax.ShapeDtypeStruct((t, d), F32),
        scratch_shapes=[pltpu.VMEM((LN_ROWS, d), F32), pltpu.VMEM((LN_ROWS, d), F32)],
        compiler_params=_params("arbitrary"),
        name="final_norm",
    )(x1, ff, mod4, g)


def kernel(x, c, positions, norm1_g, norm2_g, normf_g, w_mod, b_mod, w_in, conv_dw_w, conv_dw_b,
           conv_ln_g, conv_ln_b, ret_gn_g, w_out, w_ffn1, w_ffn3, w_ffn2):
    batch, seq, d = x.shape
    depth = w_mod.shape[0]
    t = batch * seq
    conv_ch = conv_dw_w.shape[-1]
    ret_w = ret_gn_g.shape[-1]
    hd = ret_w // RET_HEADS

    half = hd // 2
    freqs = (ROPE_BASE ** (-jnp.arange(half, dtype=F32) / half)).reshape(1, half)
    pos_b = jnp.broadcast_to(positions.astype(F32).reshape(t, 1), (t, half))
    cos_t, sin_t = _rope_tables(pos_b, freqs)
    dec, xi_b, zeta_b, cd_b = _retention_tables(hd)

    c_pad = jnp.pad(c, ((0, SUBLANES - batch % SUBLANES), (0, 0))) if batch % SUBLANES else c
    x2 = x.reshape(t, d)
    row = lambda v: v.reshape(1, -1)

    for l in range(depth):
        mod = _modulation(c_pad, w_mod[l], row(b_mod[l]))
        mod4 = mod[:batch].reshape(batch, N_MOD, 1, d)

        p, conv_out = _inproj_conv(x2, row(norm1_g[l]), mod4, 1, 0, w_in[l].astype(BF16),
                                   conv_dw_w[l], row(conv_dw_b[l]), row(conv_ln_g[l]),
                                   row(conv_ln_b[l]), conv_ch, seq, tm=IN_PROJ_TM, tn=IN_PROJ_TN)
        ret_out = _retention_group(p, cos_t, sin_t, dec, xi_b, zeta_b, cd_b, row(ret_gn_g[l]),
                                   batch, seq, 0, hd)

        x2 = _outproj(conv_out, ret_out, w_out[l].astype(BF16), x2, mod4, 2, seq)

        ff = _ffn(x2, row(norm2_g[l]), mod4, 4, 3, w_ffn1[l].astype(BF16), w_ffn3[l].astype(BF16),
                  w_ffn2[l].astype(BF16), seq, tm=FFN_TM)
        x2 = _residual(x2, ff, mod4, 5, row(normf_g), seq, final_norm=(l + 1 == depth))

    return x2.reshape(batch, seq, d)
```

```python
import functools
import math

import jax
import jax.numpy as jnp
from jax import lax
from jax.experimental import pallas as pl
from jax.experimental.pallas import tpu as pltpu

F32 = jnp.float32
BF16 = jnp.bfloat16

N_MOD = 6
CONV_WIDTH = 31
RET_HEADS = 8
ROPE_BASE = 10000.0
EPS = 1e-6
LN_EPS = 1e-5

LANES = 128
SUBLANES = 8
VMEM_LIMIT_BYTES = 56 * 1024 * 1024

RET_CHUNK = 256
CONV_HALO = 32
CONV_TILE_ROWS = 32
LN_ROWS = 16
IN_PROJ_TM = 1024
IN_PROJ_TN = 1024
IN_PROJ_STRIP = 128
FFN_TM = 1024
FFN_STRIP = 32


def _params(*semantics):
    return pltpu.CompilerParams(dimension_semantics=semantics, vmem_limit_bytes=VMEM_LIMIT_BYTES)


def _silu(x):
    return x * jax.nn.sigmoid(x)


def _mod_kernel(c_ref, w_ref, b_ref, o_ref):
    ca = _silu(c_ref[...]).astype(BF16)
    o_ref[...] = jnp.dot(ca, w_ref[...].astype(BF16), preferred_element_type=F32) + b_ref[...]


def _modulation(c_pad, w_mod, b_mod, tn=512):
    rows, d = c_pad.shape
    n = w_mod.shape[1]
    return pl.pallas_call(
        _mod_kernel,
        grid=(n // tn,),
        in_specs=[pl.BlockSpec((rows, d), lambda j: (0, 0)),
                  pl.BlockSpec((d, tn), lambda j: (0, j)),
                  pl.BlockSpec((1, tn), lambda j: (0, j))],
        out_specs=pl.BlockSpec((rows, tn), lambda j: (0, j)),
        out_shape=jax.ShapeDtypeStruct((rows, n), F32),
        compiler_params=_params("arbitrary"),
        name="adaln_mod",
    )(c_pad, w_mod, b_mod)


def _zero_after(x):
    s = jnp.sum(x[0:SUBLANES, :], axis=1, keepdims=True)
    bits = lax.bitcast_convert_type(jnp.broadcast_to(s[0:1, :], (1, LANES)), jnp.uint32)
    sixteen = jnp.uint32(16)
    bits = lax.shift_right_logical(lax.shift_right_logical(bits, sixteen), sixteen)
    return lax.bitcast_convert_type(bits, F32)


def _conv_tile(win_ref, cw_ref, base, lanes, after=None):
    first = CONV_HALO - (CONV_WIDTH - 1)
    y = None
    for r in range(SUBLANES):
        off = first + r
        lo = base + (off // SUBLANES) * SUBLANES
        sub = off % SUBLANES
        span = CONV_TILE_ROWS + (SUBLANES if sub else 0)
        u = None
        for q in range((CONV_WIDTH - 1 - r) // SUBLANES + 1):
            tap = q * SUBLANES + r
            rows = slice(lo + q * SUBLANES, lo + q * SUBLANES + span)
            w_row = cw_ref[tap:tap + 1, lanes]
            if after is not None and tap == 0:
                w_row = w_row + after
            term = w_row * win_ref[rows, lanes]
            u = term if u is None else u + term
        u = u[sub:sub + CONV_TILE_ROWS, :]
        y = u if y is None else y + u
    return y


def _inproj_conv_kernel(x_ref, g_ref, scale_ref, shift_ref, w_ref, cw_ref, cb_ref, lg_ref, lb_ref,
                        p_ref, co_ref, h_scr, a_scr, win_scr, y_scr, gs_scr, sh_scr,
                        *, tm, ch, n_glu, n_strips, rps, blocks_per_seq):
    r = pl.program_id(0)
    j = pl.program_id(1)
    i = r - 1
    mm_here = r >= 1
    strip_rows = x_ref.shape[0]

    @pl.when(j == 0)
    def _():
        gs_scr[...] = jnp.broadcast_to(g_ref[...] * (1.0 + scale_ref[...]), gs_scr.shape)
        sh_scr[...] = jnp.broadcast_to(shift_ref[...], sh_scr.shape)

        @pl.when(jnp.logical_and(mm_here, i % blocks_per_seq == 0))
        def _():
            a_scr[0:CONV_HALO, :] = jnp.zeros((CONV_HALO, ch), F32)

        @pl.when(jnp.logical_and(mm_here, i % blocks_per_seq != 0))
        def _():
            a_scr[0:CONV_HALO, :] = a_scr[tm:tm + CONV_HALO, :]

    def norm_strip():
        for k in range(strip_rows // LN_ROWS):
            x = x_ref[k * LN_ROWS:(k + 1) * LN_ROWS, :]
            inv = lax.rsqrt(jnp.mean(x * x, axis=-1, keepdims=True) + EPS)
            h = (x * inv * gs_scr[...] + sh_scr[...]).astype(h_scr.dtype)
            row0 = pl.multiple_of(j * strip_rows + k * LN_ROWS, LN_ROWS)
            h_scr[r % 2, pl.ds(row0, LN_ROWS), :] = h

    def lhs():
        return h_scr[(r + 1) % 2]

    body = slice(CONV_HALO, CONV_HALO + tm)
    tn = w_ref.shape[1]
    n_val = n_glu // 2

    @pl.when(jnp.logical_and(jnp.logical_not(mm_here), j < n_strips))
    def _():
        norm_strip()

    half = tn // 2

    def tile_dot(c):
        return jnp.dot(lhs(), w_ref[:, c * half:(c + 1) * half], preferred_element_type=F32)

    @pl.when(jnp.logical_and(mm_here, j < n_val))
    def _():
        for c in range(2):
            cols = pl.ds(pl.multiple_of(j * tn + c * half, half), half)
            a_scr[body, cols] = tile_dot(c)
        norm_strip()

    @pl.when(jnp.logical_and(mm_here, jnp.logical_and(j >= n_val, j < n_glu)))
    def _():
        for c in range(2):
            cols = pl.ds(pl.multiple_of((j - n_val) * tn + c * half, half), half)
            a_scr[body, cols] = a_scr[body, cols] * jax.nn.sigmoid(tile_dot(c))
        norm_strip()

    def conv_step(with_strip):
        r0 = pl.multiple_of((j - n_glu) * rps, rps)
        win_scr[...] = a_scr[pl.ds(r0, rps + CONV_HALO), :]

        for c in range(2):
            p_ref[:, c * half:(c + 1) * half] = tile_dot(c).astype(p_ref.dtype)

        after = None
        for c in range(ch // LANES):
            lanes = slice(c * LANES, (c + 1) * LANES)
            for rt in range(rps // CONV_TILE_ROWS):
                base = rt * CONV_TILE_ROWS
                y = _conv_tile(win_scr, cw_ref, base, lanes, after) + cb_ref[:, lanes]
                y_scr[base:base + CONV_TILE_ROWS, lanes] = y
                after = _zero_after(y)

        for k in range(rps // LN_ROWS):
            y = y_scr[k * LN_ROWS:(k + 1) * LN_ROWS, :]
            mu = jnp.mean(y, axis=-1, keepdims=True)
            d = y - mu
            var = jnp.mean(d * d, axis=-1, keepdims=True)
            z = d * lax.rsqrt(var + LN_EPS) * lg_ref[...] + lb_ref[...]
            co_ref[k * LN_ROWS:(k + 1) * LN_ROWS, :] = _silu(z).astype(co_ref.dtype)

        if with_strip:
            norm_strip()

    @pl.when(jnp.logical_and(mm_here, jnp.logical_and(j >= n_glu, j < n_strips)))
    def _():
        conv_step(with_strip=True)

    @pl.when(jnp.logical_and(mm_here, j >= max(n_glu, n_strips)))
    def _():
        conv_step(with_strip=False)


def _inproj_conv(x2, g, mod4, scale_idx, shift_idx, w, dw_w, dw_b, ln_g, ln_b, ch, seq,
                 tm=1024, tn=1024):
    t, d = x2.shape
    n = w.shape[1]
    n_glu = 2 * ch // tn
    n_col = n // tn
    n_conv = n_col - n_glu
    rps = tm // n_conv
    n_blocks = t // tm
    n_strips = tm // IN_PROJ_STRIP
    per_b = seq // tm
    assert rps % CONV_TILE_ROWS == 0 and seq % tm == 0 and ch % tn == 0
    assert n_glu <= n_strips <= n_col
    kern = functools.partial(_inproj_conv_kernel, tm=tm, ch=ch, n_glu=n_glu, n_strips=n_strips,
                             rps=rps, blocks_per_seq=per_b)
    vec = lambda: pl.BlockSpec((1, ch), lambda r, j: (0, 0))
    blk = lambda r: jnp.minimum(r, n_blocks - 1)
    modvec = lambda idx: pl.BlockSpec((None, None, 1, d), lambda r, j: (blk(r) // per_b, idx, 0, 0))
    col = lambda r, j: jnp.where(r == 0, 0, jnp.maximum(j - n_glu, 0))
    return pl.pallas_call(
        kern,
        grid=(n_blocks + 1, n_col),
        in_specs=[pl.BlockSpec((IN_PROJ_STRIP, d),
                               lambda r, j: (jnp.where(r < n_blocks,
                                                       r * n_strips + jnp.minimum(j, n_strips - 1),
                                                       n_blocks * n_strips - 1), 0)),
                  pl.BlockSpec((1, d), lambda r, j: (0, 0)),
                  modvec(scale_idx), modvec(shift_idx),
                  pl.BlockSpec((d, tn), lambda r, j: (0, jnp.where(r == 0, 0, j))),
                  pl.BlockSpec((CONV_WIDTH, ch), lambda r, j: (0, 0)),
                  vec(), vec(), vec()],
        out_specs=[pl.BlockSpec((tm, tn), lambda r, j: (jnp.maximum(r - 1, 0), col(r, j))),
                   pl.BlockSpec((rps, ch),
                                lambda r, j: (jnp.maximum(r - 1, 0) * n_conv + col(r, j), 0))],
        out_shape=[jax.ShapeDtypeStruct((t, n - 2 * ch), BF16),
                   jax.ShapeDtypeStruct((t, ch), BF16)],
        scratch_shapes=[pltpu.VMEM((2, tm, d), BF16),
                        pltpu.VMEM((CONV_HALO + tm, ch), F32),
                        pltpu.VMEM((rps + CONV_HALO, ch), F32),
                        pltpu.VMEM((rps, ch), F32),
                        pltpu.VMEM((LN_ROWS, d), F32),
                        pltpu.VMEM((LN_ROWS, d), F32)],
        compiler_params=_params("arbitrary", "arbitrary"),
        name="in_proj_conv",
    )(x2, g, mod4, mod4, w, dw_w, dw_b, ln_g, ln_b)


def _rope_table_kernel(pos_ref, freq_ref, cos_ref, sin_ref):
    ang = pos_ref[...] * freq_ref[...]
    cos_ref[...] = jnp.cos(ang)
    sin_ref[...] = jnp.sin(ang)


def _rope_tables(pos_b, freqs, ts=512):
    t, half = pos_b.shape
    blk = lambda: pl.BlockSpec((ts, half), lambda i: (i, 0))
    return pl.pallas_call(
        _rope_table_kernel,
        grid=(t // ts,),
        in_specs=[blk(), pl.BlockSpec((1, half), lambda i: (0, 0))],
        out_specs=[blk(), blk()],
        out_shape=[jax.ShapeDtypeStruct((t, half), F32)] * 2,
        compiler_params=_params("arbitrary"),
        name="rope_tables",
    )(pos_b, freqs)


def _ret_kernel(q_ref, k_ref, v_ref, g_ref, cos_ref, sin_ref, dec_ref, xi_ref, zeta_ref, cd_ref,
                gn_ref, o_ref, state_ref, *, heads, hd, kscale):
    @pl.when(pl.program_id(1) == 0)
    def _():
        state_ref[...] = jnp.zeros(state_ref.shape, F32)

    half = hd // 2
    cos = cos_ref[...]
    sin = sin_ref[...]

    def rope(t_ref, lo):
        t1 = t_ref[:, lo:lo + half].astype(F32)
        t2 = t_ref[:, lo + half:lo + hd].astype(F32)
        return jnp.concatenate([t1 * cos - t2 * sin, t2 * cos + t1 * sin], axis=-1)

    for hh in range(heads):
        lo = hh * hd
        cols = slice(lo, lo + hd)
        qr = rope(q_ref, lo)
        kr = rope(k_ref, lo) * kscale
        v = v_ref[:, cols]

        scores = lax.dot_general(qr.astype(BF16), kr.astype(BF16), (((1,), (1,)), ((), ())),
                                 preferred_element_type=F32) * dec_ref[hh]
        inner = jnp.dot(scores.astype(BF16), v, preferred_element_type=F32)

        state = state_ref[hh]
        cross = jnp.dot((qr * xi_ref[hh]).astype(BF16), state.astype(BF16),
                        preferred_element_type=F32)
        kz_t = jnp.transpose(kr * zeta_ref[hh]).astype(BF16)
        state_ref[hh] = state * cd_ref[hh] + jnp.dot(kz_t, v, preferred_element_type=F32)

        out = inner + cross
        mu = jnp.mean(out, axis=-1, keepdims=True)
        d = out - mu
        var = jnp.mean(d * d, axis=-1, keepdims=True)
        y = d * lax.rsqrt(var + LN_EPS) * gn_ref[:, cols]
        o_ref[:, cols] = (_silu(g_ref[:, cols].astype(F32)) * y).astype(o_ref.dtype)


def _retention_group(p, cos_t, sin_t, dec, xi_b, zeta_b, cd_b, gn_g, batch, seq, col0, hd):
    t = p.shape[0]
    c = RET_CHUNK
    nc = seq // c
    h = RET_HEADS
    w = h * hd
    first = col0 // w
    row = lambda b, n: b * nc + n
    group_blk = lambda i: pl.BlockSpec((c, w), lambda b, n: (row(b, n), first + i))
    whole = lambda a: pl.BlockSpec(a.shape, lambda b, n: (0,) * a.ndim)
    rope_blk = lambda: pl.BlockSpec((c, hd // 2), lambda b, n: (row(b, n), 0))
    kern = functools.partial(_ret_kernel, heads=h, hd=hd, kscale=float(hd) ** -0.5)
    return pl.pallas_call(
        kern,
        grid=(batch, nc),
        in_specs=[group_blk(0), group_blk(1), group_blk(2), group_blk(3),
                  rope_blk(), rope_blk(),
                  whole(dec), whole(xi_b), whole(zeta_b), whole(cd_b), whole(gn_g)],
        out_specs=pl.BlockSpec((c, w), lambda b, n: (row(b, n), 0)),
        out_shape=jax.ShapeDtypeStruct((t, w), BF16),
        scratch_shapes=[pltpu.VMEM((h, hd, hd), F32)],
        compiler_params=_params("arbitrary", "arbitrary"),
        name="retention_group",
    )(p, p, p, p, cos_t, sin_t, dec, xi_b, zeta_b, cd_b, gn_g)


def _retention_tables(hd):
    c = RET_CHUNK
    heads = jnp.arange(RET_HEADS, dtype=F32)
    log_gamma = jnp.log(1.0 - 2.0 ** (-5.0 - heads))
    idx = jnp.arange(c)
    idx_f = idx.astype(F32)
    diff = (idx[:, None] - idx[None, :]).astype(F32)
    dec = jnp.where(diff[None] >= 0,
                    jnp.exp(jnp.maximum(diff, 0.0)[None] * log_gamma[:, None, None]), 0.0)
    xi = jnp.exp((idx_f + 1.0)[None] * log_gamma[:, None])
    zeta = jnp.exp((c - 1.0 - idx_f)[None] * log_gamma[:, None])
    cd = jnp.exp(c * log_gamma)
    bcast = lambda a: jnp.broadcast_to(a[:, :, None], (RET_HEADS, c, hd))
    return dec, bcast(xi), bcast(zeta), jnp.broadcast_to(cd[:, None, None], (RET_HEADS, 1, hd))


def _outproj_kernel(a1_ref, a2_ref, w1_ref, w2_ref, x_ref, gate_ref, o_ref):
    mix = (jnp.dot(a1_ref[...], w1_ref[...], preferred_element_type=F32)
           + jnp.dot(a2_ref[...], w2_ref[...], preferred_element_type=F32))
    o_ref[...] = x_ref[...] + gate_ref[...] * mix


def _outproj(a1, a2, w, x2, mod4, gate_idx, seq, tm=1024, tn=512):
    m, k1 = a1.shape
    k2 = a2.shape[1]
    n = w.shape[1]
    per_b = seq // tm
    return pl.pallas_call(
        _outproj_kernel,
        grid=(m // tm, n // tn),
        in_specs=[pl.BlockSpec((tm, k1), lambda i, j: (i, 0)),
                  pl.BlockSpec((tm, k2), lambda i, j: (i, 0)),
                  pl.BlockSpec((k1, tn), lambda i, j: (0, j)),
                  pl.BlockSpec((k2, tn), lambda i, j: (k1 // k2, j)),
                  pl.BlockSpec((tm, tn), lambda i, j: (i, j)),
                  pl.BlockSpec((None, None, 1, tn), lambda i, j: (i // per_b, gate_idx, 0, j))],
        out_specs=pl.BlockSpec((tm, tn), lambda i, j: (i, j)),
        out_shape=jax.ShapeDtypeStruct((m, n), F32),
        compiler_params=_params("arbitrary", "arbitrary"),
        name="out_proj",
    )(a1, a2, w, w, x2, mod4)


def _ffn_kernel(x1_ref, g_ref, scale_ref, shift_ref, w1_ref, w3_ref, w2_ref, o_ref,
                h_scr, acc_ref, gs_scr, sh_scr, u_scr, *, n_blocks, n_strips):
    r = pl.program_id(0)
    f = pl.program_id(1)
    norm_here = jnp.logical_and(r < n_blocks, f < n_strips)
    mm_here = r >= 1

    @pl.when(jnp.logical_and(r < n_blocks, f == 0))
    def _():
        gs_scr[...] = jnp.broadcast_to(g_ref[...] * (1.0 + scale_ref[...]), gs_scr.shape)
        sh_scr[...] = jnp.broadcast_to(shift_ref[...], sh_scr.shape)

    @pl.when(jnp.logical_and(mm_here, f == 0))
    def _():
        acc_ref[...] = jnp.zeros(acc_ref.shape, F32)

    def norm_strip():
        for k in range(FFN_STRIP // LN_ROWS):
            x = x1_ref[k * LN_ROWS:(k + 1) * LN_ROWS, :]
            inv = lax.rsqrt(jnp.mean(x * x, axis=-1, keepdims=True) + EPS)
            h = (x * inv * gs_scr[...] + sh_scr[...]).astype(h_scr.dtype)
            row0 = pl.multiple_of(f * FFN_STRIP + k * LN_ROWS, LN_ROWS)
            h_scr[r % 2, pl.ds(row0, LN_ROWS), :] = h

    last = pl.num_programs(1) - 1

    def down():
        acc_ref[...] += jnp.dot(u_scr[...], w2_ref[...], preferred_element_type=F32)

    def up():
        h = h_scr[(r + 1) % 2]
        a = jnp.dot(h, w1_ref[...], preferred_element_type=F32)
        b = jnp.dot(h, w3_ref[...], preferred_element_type=F32)
        u_scr[...] = (_silu(a) * b).astype(u_scr.dtype)

    @pl.when(jnp.logical_and(norm_here, jnp.logical_not(mm_here)))
    def _():
        norm_strip()

    @pl.when(jnp.logical_and(mm_here, f == 0))
    def _():
        up()
        norm_strip()

    @pl.when(jnp.logical_and(mm_here, jnp.logical_and(f > 0, f < n_strips)))
    def _():
        down()
        up()
        norm_strip()

    @pl.when(jnp.logical_and(mm_here, jnp.logical_and(f >= n_strips, f < last)))
    def _():
        down()
        up()

    @pl.when(jnp.logical_and(mm_here, f == last))
    def _():
        down()
        o_ref[...] = acc_ref[...].astype(o_ref.dtype)


def _ffn(x1, g, mod4, scale_idx, shift_idx, w1, w3, w2, seq, tm=1024, tf=256):
    t, d = x1.shape
    nf = w1.shape[1] // tf
    n_blocks = t // tm
    n_strips = tm // FFN_STRIP
    per_b = seq // tm
    assert nf >= n_strips and seq % tm == 0
    last_strip = n_blocks * n_strips - 1
    blk = lambda r: jnp.minimum(r, n_blocks - 1)
    modvec = lambda idx: pl.BlockSpec((None, None, 1, d), lambda r, f: (blk(r) // per_b, idx, 0, 0))
    wcol = lambda r, f: (0, jnp.where(r == 0, 0, jnp.minimum(f, nf - 1)))
    kern = functools.partial(_ffn_kernel, n_blocks=n_blocks, n_strips=n_strips)
    return pl.pallas_call(
        kern,
        grid=(n_blocks + 1, nf + 1),
        in_specs=[pl.BlockSpec((FFN_STRIP, d),
                               lambda r, f: (jnp.where(r < n_blocks,
                                                       r * n_strips + jnp.minimum(f, n_strips - 1),
                                                       last_strip), 0)),
                  pl.BlockSpec((1, d), lambda r, f: (0, 0)),
                  modvec(scale_idx), modvec(shift_idx),
                  pl.BlockSpec((d, tf), wcol),
                  pl.BlockSpec((d, tf), wcol),
                  pl.BlockSpec((tf, d), lambda r, f: (jnp.where(r == 0, 0, jnp.maximum(f - 1, 0)), 0))],
        out_specs=pl.BlockSpec((tm, d), lambda r, f: (jnp.maximum(r - 1, 0), 0),
                               pipeline_mode=pl.Buffered(1)),
        out_shape=jax.ShapeDtypeStruct((t, d), BF16),
        scratch_shapes=[pltpu.VMEM((2, tm, d), BF16),
                        pltpu.VMEM((tm, d), F32),
                        pltpu.VMEM((LN_ROWS, d), F32),
                        pltpu.VMEM((LN_ROWS, d), F32),
                        pltpu.VMEM((tm, tf), BF16)],
        compiler_params=_params("arbitrary", "arbitrary"),
        name="swiglu_ffn",
    )(x1, g, mod4, mod4, w1, w3, w2)


def _residual_kernel(x_ref, ff_ref, gate_ref, g_ref, o_ref, gate_scr, g_scr, *, final_norm):
    strip = gate_scr.shape
    gate_scr[...] = jnp.broadcast_to(gate_ref[...], strip)
    g_scr[...] = jnp.broadcast_to(g_ref[...], strip)
    for r in range(x_ref.shape[0] // LN_ROWS):
        rows = slice(r * LN_ROWS, (r + 1) * LN_ROWS)
        y = x_ref[rows, :] + gate_scr[...] * ff_ref[rows, :].astype(F32)
        if final_norm:
            inv = lax.rsqrt(jnp.mean(y * y, axis=-1, keepdims=True) + EPS)
            y = y * inv * g_scr[...]
        o_ref[rows, :] = y


def _residual(x1, ff, mod4, gate_idx, g, seq, final_norm, ts=256):
    t, d = x1.shape
    per_b = seq // ts
    return pl.pallas_call(
        functools.partial(_residual_kernel, final_norm=final_norm),
        grid=(t // ts,),
        in_specs=[pl.BlockSpec((ts, d), lambda i: (i, 0)),
                  pl.BlockSpec((ts, d), lambda i: (i, 0)),
                  pl.BlockSpec((None, None, 1, d), lambda i: (i // per_b, gate_idx, 0, 0)),
                  pl.BlockSpec((1, d), lambda i: (0, 0))],
        out_specs=pl.BlockSpec((ts, d), lambda i: (i, 0)),
        out_shape=jax.ShapeDtypeStruct((t, d), F32),
        scratch_shapes=[pltpu.VMEM((LN_ROWS, d), F32), pltpu.VMEM((LN_ROWS, d), F32)],
        compiler_params=_params("arbitrary"),
        name="final_norm",
    )(x1, ff, mod4, g)


def kernel(x, c, positions, norm1_g, norm2_g, normf_g, w_mod, b_mod, w_in, conv_dw_w, conv_dw_b,
           conv_ln_g, conv_ln_b, ret_gn_g, w_out, w_ffn1, w_ffn3, w_ffn2):
    batch, seq, d = x.shape
    depth = w_mod.shape[0]
    t = batch * seq
    conv_ch = conv_dw_w.shape[-1]
    ret_w = ret_gn_g.shape[-1]
    hd = ret_w // RET_HEADS

    half = hd // 2
    freqs = (ROPE_BASE ** (-jnp.arange(half, dtype=F32) / half)).reshape(1, half)
    pos_b = jnp.broadcast_to(positions.astype(F32).reshape(t, 1), (t, half))
    cos_t, sin_t = _rope_tables(pos_b, freqs)
    dec, xi_b, zeta_b, cd_b = _retention_tables(hd)

    c_pad = jnp.pad(c, ((0, SUBLANES - batch % SUBLANES), (0, 0))) if batch % SUBLANES else c
    x2 = x.reshape(t, d)
    row = lambda v: v.reshape(1, -1)

    for l in range(depth):
        mod = _modulation(c_pad, w_mod[l], row(b_mod[l]))
        mod4 = mod[:batch].reshape(batch, N_MOD, 1, d)

        p, conv_out = _inproj_conv(x2, row(norm1_g[l]), mod4, 1, 0, w_in[l].astype(BF16),
                                   conv_dw_w[l], row(conv_dw_b[l]), row(conv_ln_g[l]),
                                   row(conv_ln_b[l]), conv_ch, seq, tm=IN_PROJ_TM, tn=IN_PROJ_TN)
        ret_out = _retention_group(p, cos_t, sin_t, dec, xi_b, zeta_b, cd_b, row(ret_gn_g[l]),
                                   batch, seq, 0, hd)

        x2 = _outproj(conv_out, ret_out, w_out[l].astype(BF16), x2, mod4, 2, seq)

        ff = _ffn(x2, row(norm2_g[l]), mod4, 4, 3, w_ffn1[l].astype(BF16), w_ffn3[l].astype(BF16),
                  w_ffn2[l].astype(BF16), seq, tm=FFN_TM)
        x2 = _residual(x2, ff, mod4, 5, row(normf_g), seq, final_norm=(l + 1 == depth))

    return x2.reshape(batch, seq, d)
```
